```python
import math
import jax
import jax.numpy as jnp
from jax import lax
import numpy as np

D_MODEL = 1024
BATCH = 4
SEQ = 4096
DEPTH = 4

N_MIXERS = 3
Q_BLOCK = 128
NORM_EPS = 1e-6
FFN_HIDDEN = -(-(8 * D_MODEL) // (3 * 256)) * 256

DN_HEAD_DIM = 128
DN_HEADS = D_MODEL // DN_HEAD_DIM
DN_WIDTH = DN_HEADS * DN_HEAD_DIM
DN_CONV = 4
DN_CHUNK = 64

SB_HEAD_DIM = 128
SB_HEADS = D_MODEL // SB_HEAD_DIM

MLA_HEADS = D_MODEL // 128
MLA_NOPE = 128
MLA_ROPE = 64
MLA_QK = MLA_NOPE + MLA_ROPE
MLA_V = 128
MLA_Q_RANK = 256
MLA_KV_RANK = 128
ROPE_THETA = 10000.0

kernel_name = 'hybrid_deltanet_stickbreak_mla_trunk'


def _rmsnorm(x, g):
    xf = x.astype(jnp.float32)
    y = xf * lax.rsqrt(jnp.mean(xf * xf, axis=-1, keepdims=True) + NORM_EPS)
    return (y * g.astype(jnp.float32)).astype(x.dtype)


def _l2norm(x):
    return x * lax.rsqrt(jnp.sum(x * x, axis=-1, keepdims=True) + NORM_EPS)


def _heads_first(x):
    return x.transpose(0, 2, 1, 3)


def _swiglu(h, w_gate_up, w_down):
    gate, up = jnp.split(h @ w_gate_up, 2, axis=-1)
    return (jax.nn.silu(gate) * up) @ w_down


def _causal_dwconv(u, w):
    k_width, t = w.shape[0], u.shape[1]
    up = jnp.pad(u, ((0, 0), (k_width - 1, 0), (0, 0)))
    y = up[:, 0:t] * w[0]
    for j in range(1, k_width):
        y = y + up[:, j:j + t] * w[j]
    return y


def _to_chunks(x):
    b, t, h = x.shape[:3]
    return jnp.moveaxis(x.reshape(b, t // DN_CHUNK, DN_CHUNK, h, *x.shape[3:]), 3, 1)


def _chunk_gated_delta_rule(q, k, v, g, beta):
    b, t, h, dk = q.shape
    dv = v.shape[-1]
    qc, kc, vc = _to_chunks(q), _to_chunks(k), _to_chunks(v)
    gc = jnp.cumsum(_to_chunks(g), axis=-1)
    bc = _to_chunks(beta)[..., None]
    idx = jnp.arange(DN_CHUNK)
    causal = idx[:, None] >= idx[None, :]
    strict = idx[:, None] > idx[None, :]
    diff = gc[..., :, None] - gc[..., None, :]
    decay = jnp.where(causal, jnp.exp(jnp.where(causal, diff, 0.0)), 0.0)
    k_beta = kc * bc
    lower = jnp.where(strict, jnp.einsum('bhncd,bhnsd->bhncs', k_beta, kc) * decay, 0.0)
    tmat = lower + jnp.eye(DN_CHUNK, dtype=lower.dtype)
    u = lax.linalg.triangular_solve(tmat, vc * bc, left_side=True, lower=True, unit_diagonal=True)
    w = lax.linalg.triangular_solve(tmat, k_beta * jnp.exp(gc)[..., None], left_side=True, lower=True, unit_diagonal=True)
    attn = jnp.einsum('bhncd,bhnsd->bhncs', qc, kc) * decay
    q_dec = qc * jnp.exp(gc)[..., None]
    k_dec = kc * jnp.exp(gc[..., -1:] - gc)[..., None]
    chunk_decay = jnp.exp(gc[..., -1])[..., None, None]

    def step(state, inp):
        u_i, w_i, q_i, k_i, a_i, d_i = inp
        v_new = u_i - jnp.einsum('bhcd,bhde->bhce', w_i, state)
        o_i = jnp.einsum('bhcd,bhde->bhce', q_i, state) + jnp.einsum('bhcs,bhse->bhce', a_i, v_new)
        state = state * d_i + jnp.einsum('bhcd,bhce->bhde', k_i, v_new)
        return state, o_i

    xs = tuple(jnp.moveaxis(a, 2, 0) for a in (u, w, q_dec, k_dec, attn, chunk_decay))
    state0 = jnp.zeros((b, h, dk, dv), q.dtype)
    _, o = lax.scan(step, state0, xs)
    return jnp.moveaxis(o, 0, 2).reshape(b, h, t, dv).transpose(0, 2, 1, 3)


def _gated_deltanet_mixer(h, w_in, conv_w, a_log, dt_bias, out_norm, w_out):
    b, t, _ = h.shape
    proj = h @ w_in
    qkv = jax.nn.silu(_causal_dwconv(proj[..., :3 * DN_WIDTH], conv_w))
    z = proj[..., 3 * DN_WIDTH:4 * DN_WIDTH].reshape(b, t, DN_HEADS, DN_HEAD_DIM).astype(jnp.float32)
    a = proj[..., 4 * DN_WIDTH:4 * DN_WIDTH + DN_HEADS].astype(jnp.float32)
    bl = proj[..., 4 * DN_WIDTH + DN_HEADS:].astype(jnp.float32)
    q, k, v = (y.reshape(b, t, DN_HEADS, DN_HEAD_DIM).astype(jnp.float32) for y in jnp.split(qkv, 3, axis=-1))
    q = _l2norm(q) * (DN_HEAD_DIM ** -0.5)
    k = _l2norm(k)
    g = -jnp.exp(a_log.astype(jnp.float32)) * jax.nn.softplus(a + dt_bias.astype(jnp.float32))
    beta = jax.nn.sigmoid(bl)
    o = _chunk_gated_delta_rule(q, k, v, g, beta)
    o = _rmsnorm(o, out_norm) * jax.nn.silu(z)
    return o.reshape(b, t, DN_WIDTH).astype(h.dtype) @ w_out


def _stick_breaking_weights(z, tpos, spos):
    past = spos < tpos
    log_stay = jnp.where(past, jax.nn.log_sigmoid(-z), 0.0)
    log_after = lax.cumsum(log_stay, axis=z.ndim - 1, reverse=True) - log_stay
    return jnp.where(past, jnp.exp(jax.nn.log_sigmoid(z) + log_after), 0.0)


def _softmax_weights(z, tpos, spos):
    return jax.nn.softmax(jnp.where(spos <= tpos, z, -jnp.inf), axis=-1)


def _causal_block_sweep(q, k, v, weight_fn):
    t = q.shape[2]
    local = jnp.arange(Q_BLOCK)
    outs = []
    for blk in range(t // Q_BLOCK):
        start, end = blk * Q_BLOCK, (blk + 1) * Q_BLOCK
        z = jnp.einsum('bhqd,bhkd->bhqk', q[:, :, start:end], k[:, :, :end])
        tpos = (start + local)[:, None]
        spos = jnp.arange(end)[None, :]
        wts = weight_fn(z, tpos, spos)
        outs.append(jnp.einsum('bhqk,bhkd->bhqd', wts, v[:, :, :end]))
    return jnp.concatenate(outs, axis=2)


def _stick_breaking_mixer(h, w_qkv, q_norm, k_norm, w_out):
    b, t, _ = h.shape
    q, k, v = (y.reshape(b, t, SB_HEADS, SB_HEAD_DIM) for y in jnp.split(h @ w_qkv, 3, axis=-1))
    q = _rmsnorm(q, q_norm).astype(jnp.float32) * (SB_HEAD_DIM ** -0.5)
    k = _rmsnorm(k, k_norm).astype(jnp.float32)
    o = _causal_block_sweep(_heads_first(q), _heads_first(k), _heads_first(v.astype(jnp.float32)), _stick_breaking_weights)
    return _heads_first(o).reshape(b, t, SB_HEADS * SB_HEAD_DIM).astype(h.dtype) @ w_out


def _rope_tables(t):
    inv_freq = ROPE_THETA ** (-jnp.arange(0, MLA_ROPE, 2, dtype=jnp.float32) / MLA_ROPE)
    ang = jnp.arange(t, dtype=jnp.float32)[:, None] * inv_freq[None, :]
    return jnp.cos(ang), jnp.sin(ang)


def _apply_rope(x, cos, sin):
    x1, x2 = jnp.split(x.astype(jnp.float32), 2, axis=-1)
    c, s = cos[None, :, None, :], sin[None, :, None, :]
    return jnp.concatenate([x1 * c - x2 * s, x2 * c + x1 * s], axis=-1)


def _mla_mixer(h, w_down, q_a_norm, kv_a_norm, w_uq, w_ukv, q_nope_norm, q_rope_norm, k_nope_norm, k_rope_norm, w_out):
    b, t, _ = h.shape
    c_q, c_kv, k_rope = jnp.split(h @ w_down, [MLA_Q_RANK, MLA_Q_RANK + MLA_KV_RANK], axis=-1)
    q = (_rmsnorm(c_q, q_a_norm) @ w_uq).reshape(b, t, MLA_HEADS, MLA_QK)
    kv = (_rmsnorm(c_kv, kv_a_norm) @ w_ukv).reshape(b, t, MLA_HEADS, MLA_NOPE + MLA_V)
    v = kv[..., MLA_NOPE:]
    cos, sin = _rope_tables(t)
    q_nope = _rmsnorm(q[..., :MLA_NOPE], q_nope_norm).astype(jnp.float32)
    q_rot = _apply_rope(_rmsnorm(q[..., MLA_NOPE:], q_rope_norm), cos, sin)
    q = jnp.concatenate([q_nope, q_rot], axis=-1) * (MLA_QK ** -0.5)
    k_nope = _rmsnorm(kv[..., :MLA_NOPE], k_nope_norm).astype(jnp.float32)
    k_rot = _apply_rope(_rmsnorm(k_rope, k_rope_norm)[:, :, None, :], cos, sin)
    k = jnp.concatenate([k_nope, jnp.broadcast_to(k_rot, (b, t, MLA_HEADS, MLA_ROPE))], axis=-1)
    o = _causal_block_sweep(_heads_first(q), _heads_first(k), _heads_first(v.astype(jnp.float32)), _softmax_weights)
    return _heads_first(o).reshape(b, t, MLA_HEADS * MLA_V).astype(h.dtype) @ w_out


def _dense(key, fan_in, fan_out):
    return jax.random.normal(key, (fan_in, fan_out), jnp.float32) * (fan_in ** -0.5)


def _gain(key, n):
    return 1.0 + 0.02 * jax.random.normal(key, (n,), jnp.float32)


def _deltanet_params(key, p):
    ks = jax.random.split(key, 6)
    dt = jnp.exp(jax.random.uniform(ks[3], (DN_HEADS,), jnp.float32, math.log(1e-3), math.log(1e-1)))
    return {
        p + 'dn_w_in': _dense(ks[0], D_MODEL, 4 * DN_WIDTH + 2 * DN_HEADS),
        p + 'dn_conv_w': jax.random.normal(ks[1], (DN_CONV, 3 * DN_WIDTH), jnp.float32) * (DN_CONV ** -0.5),
        p + 'dn_a_log': jnp.log(jax.random.uniform(ks[2], (DN_HEADS,), jnp.float32, 1.0, 16.0)),
        p + 'dn_dt_bias': dt + jnp.log(-jnp.expm1(-dt)),
        p + 'dn_out_norm': _gain(ks[4], DN_HEAD_DIM),
        p + 'dn_w_out': _dense(ks[5], DN_WIDTH, D_MODEL),
    }


def _stick_breaking_params(key, p):
    ks = jax.random.split(key, 4)
    return {
        p + 'sb_w_qkv': _dense(ks[0], D_MODEL, 3 * SB_HEADS * SB_HEAD_DIM),
        p + 'sb_q_norm': _gain(ks[1], SB_HEAD_DIM),
        p + 'sb_k_norm': _gain(ks[2], SB_HEAD_DIM),
        p + 'sb_w_out': _dense(ks[3], SB_HEADS * SB_HEAD_DIM, D_MODEL),
    }


def _mla_params(key, p):
    ks = jax.random.split(key, 10)
    return {
        p + 'mla_w_down': _dense(ks[0], D_MODEL, MLA_Q_RANK + MLA_KV_RANK + MLA_ROPE),
        p + 'mla_q_a_norm': _gain(ks[1], MLA_Q_RANK),
        p + 'mla_kv_a_norm': _gain(ks[2], MLA_KV_RANK),
        p + 'mla_w_uq': _dense(ks[3], MLA_Q_RANK, MLA_HEADS * MLA_QK),
        p + 'mla_w_ukv': _dense(ks[4], MLA_KV_RANK, MLA_HEADS * (MLA_NOPE + MLA_V)),
        p + 'mla_q_nope_norm': _gain(ks[5], MLA_NOPE),
        p + 'mla_q_rope_norm': _gain(ks[6], MLA_ROPE),
        p + 'mla_k_nope_norm': _gain(ks[7], MLA_NOPE),
        p + 'mla_k_rope_norm': _gain(ks[8], MLA_ROPE),
        p + 'mla_w_out': _dense(ks[9], MLA_HEADS * MLA_V, D_MODEL),
    }


def setup_inputs(seed: int = 0) -> dict:
    key = jax.random.key(seed)
    k_x, k_layers = jax.random.split(key)
    inputs = {'x': jax.random.normal(k_x, (BATCH, SEQ, D_MODEL), jnp.float32)}
    builders = (_deltanet_params, _stick_breaking_params, _mla_params)
    layer_keys = jax.random.split(k_layers, DEPTH)
    for i in range(DEPTH):
        ks = jax.random.split(layer_keys[i], 5)
        p = 'l' + str(i) + '_'
        inputs[p + 'mix_norm'] = _gain(ks[0], D_MODEL)
        inputs.update(builders[i % N_MIXERS](ks[1], p))
        inputs[p + 'ffn_norm'] = _gain(ks[2], D_MODEL)
        inputs[p + 'ffn_w_gate_up'] = _dense(ks[3], D_MODEL, 2 * FFN_HIDDEN)
        inputs[p + 'ffn_w_down'] = _dense(ks[4], FFN_HIDDEN, D_MODEL)
    return inputs


def reference(x,
              l0_mix_norm, l0_dn_w_in, l0_dn_conv_w, l0_dn_a_log, l0_dn_dt_bias, l0_dn_out_norm, l0_dn_w_out,
              l0_ffn_norm, l0_ffn_w_gate_up, l0_ffn_w_down,
              l1_mix_norm, l1_sb_w_qkv, l1_sb_q_norm, l1_sb_k_norm, l1_sb_w_out,
              l1_ffn_norm, l1_ffn_w_gate_up, l1_ffn_w_down,
              l2_mix_norm, l2_mla_w_down, l2_mla_q_a_norm, l2_mla_kv_a_norm, l2_mla_w_uq, l2_mla_w_ukv,
              l2_mla_q_nope_norm, l2_mla_q_rope_norm, l2_mla_k_nope_norm, l2_mla_k_rope_norm, l2_mla_w_out,
              l2_ffn_norm, l2_ffn_w_gate_up, l2_ffn_w_down,
              l3_mix_norm, l3_dn_w_in, l3_dn_conv_w, l3_dn_a_log, l3_dn_dt_bias, l3_dn_out_norm, l3_dn_w_out,
              l3_ffn_norm, l3_ffn_w_gate_up, l3_ffn_w_down):
    mixers = (_gated_deltanet_mixer, _stick_breaking_mixer, _mla_mixer)
    layers = (
        (l0_mix_norm, (l0_dn_w_in, l0_dn_conv_w, l0_dn_a_log, l0_dn_dt_bias, l0_dn_out_norm, l0_dn_w_out),
         l0_ffn_norm, l0_ffn_w_gate_up, l0_ffn_w_down),
        (l1_mix_norm, (l1_sb_w_qkv, l1_sb_q_norm, l1_sb_k_norm, l1_sb_w_out),
         l1_ffn_norm, l1_ffn_w_gate_up, l1_ffn_w_down),
        (l2_mix_norm, (l2_mla_w_down, l2_mla_q_a_norm, l2_mla_kv_a_norm, l2_mla_w_uq, l2_mla_w_ukv,
                       l2_mla_q_nope_norm, l2_mla_q_rope_norm, l2_mla_k_nope_norm, l2_mla_k_rope_norm, l2_mla_w_out),
         l2_ffn_norm, l2_ffn_w_gate_up, l2_ffn_w_down),
        (l3_mix_norm, (l3_dn_w_in, l3_dn_conv_w, l3_dn_a_log, l3_dn_dt_bias, l3_dn_out_norm, l3_dn_w_out),
         l3_ffn_norm, l3_ffn_w_gate_up, l3_ffn_w_down),
    )
    for i in range(DEPTH):
        mix_norm, mix_params, ffn_norm, w_gate_up, w_down = layers[i]
        x = x + mixers[i % N_MIXERS](_rmsnorm(x, mix_norm), *mix_params)
        x = x + _swiglu(_rmsnorm(x, ffn_norm), w_gate_up, w_down)
    return x
```

```python
import functools

import jax
import jax.numpy as jnp
from jax import lax
from jax.experimental import pallas as pl
from jax.experimental.pallas import tpu as pltpu

F32 = jnp.float32
BF16 = jnp.bfloat16

NORM_EPS = 1e-6
LANES = 128
HEAD_DIM = 128
N_HEADS = 8
DN_CONV = 4
DN_CHUNK = 128
MLA_ROPE = 64
MLA_QK = 192
MLA_Q_RANK = 256
MLA_KV_RANK = 128
MLA_QPAD = 256
ROPE_THETA = 10000.0
VMEM_LIMIT = 56 * 1024 * 1024


def _params(*sem):
    return pltpu.CompilerParams(dimension_semantics=sem, vmem_limit_bytes=VMEM_LIMIT)


def _rms(x):
    return x * lax.rsqrt(jnp.mean(x * x, axis=-1, keepdims=True) + NORM_EPS)


def _sigmoid(x):
    return 1.0 / (1.0 + jnp.exp(-x))


def _softplus(x):
    return jnp.maximum(x, 0.0) + jnp.log(1.0 + jnp.exp(-jnp.abs(x)))


def _dot(a, b):
    return jnp.dot(a, b, preferred_element_type=F32)


def _dot_nt(a, b):
    return lax.dot_general(a, b, (((1,), (1,)), ((), ())), preferred_element_type=F32)


def _split3(x):
    hi = x.astype(BF16)
    r = x - hi.astype(F32)
    mid = r.astype(BF16)
    lo = (r - mid.astype(F32)).astype(BF16)
    return hi, mid, lo


def _plain_epilogue(acc, j, extra, o_ref):
    del j, extra
    o_ref[...] = acc.astype(o_ref.dtype)


def _sb_epilogue(acc, j, extra, o_ref):
    qg_ref, kg_ref = extra

    def head_norm(gain, scale):
        for h in range(acc.shape[1] // HEAD_DIM):
            sl = slice(h * HEAD_DIM, (h + 1) * HEAD_DIM)
            o_ref[:, sl] = (_rms(acc[:, sl]) * gain * scale).astype(o_ref.dtype)

    @pl.when(j == 0)
    def _():
        head_norm(qg_ref[...], HEAD_DIM ** -0.5)

    @pl.when(j == 1)
    def _():
        head_norm(kg_ref[...], 1.0)

    @pl.when(j == 2)
    def _():
        o_ref[...] = acc.astype(o_ref.dtype)


def _norm_matmul_body(x_ref, g_ref, w_ref, *rest, epilogue, n_extra):
    extra, o_ref, h_ref = rest[:n_extra], rest[n_extra], rest[n_extra + 1]
    j = pl.program_id(1)

    @pl.when(j == 0)
    def _():
        h_ref[...] = (_rms(x_ref[...]) * g_ref[...]).astype(BF16)

    epilogue(_dot(h_ref[...], w_ref[...]), j, extra, o_ref)


def _norm_matmul(x, gain, w, *, tm, tn, out_dtype, epilogue=_plain_epilogue, extra=()):
    m, d = x.shape
    n = w.shape[1]
    tm = min(tm, m)
    extra_specs = [pl.BlockSpec(e.shape, lambda i, j: (0, 0)) for e in extra]
    return pl.pallas_call(
        functools.partial(_norm_matmul_body, epilogue=epilogue, n_extra=len(extra)),
        grid=(m // tm, n // tn),
        in_specs=[pl.BlockSpec((tm, d), lambda i, j: (i, 0)),
                  pl.BlockSpec((1, d), lambda i, j: (0, 0)),
                  pl.BlockSpec((d, tn), lambda i, j: (0, j))] + extra_specs,
        out_specs=pl.BlockSpec((tm, tn), lambda i, j: (i, j)),
        out_shape=jax.ShapeDtypeStruct((m, n), out_dtype),
        scratch_shapes=[pltpu.VMEM((tm, d), BF16)],
        compiler_params=_params("parallel", "arbitrary"),
        name="norm_matmul",
    )(x, gain.reshape(1, d), w, *extra)


def _matmul_res_body(a_ref, w_ref, r_ref, o_ref):
    o_ref[...] = r_ref[...] + _dot(a_ref[...], w_ref[...])


def _matmul_residual(a, w, res, *, tm, tn):
    m, k = a.shape
    n = w.shape[1]
    tm = min(tm, m)
    return pl.pallas_call(
        _matmul_res_body,
        grid=(m // tm, n // tn),
        in_specs=[pl.BlockSpec((tm, k), lambda i, j: (i, 0)),
                  pl.BlockSpec((k, tn), lambda i, j: (0, j)),
                  pl.BlockSpec((tm, tn), lambda i, j: (i, j))],
        out_specs=pl.BlockSpec((tm, tn), lambda i, j: (i, j)),
        out_shape=jax.ShapeDtypeStruct((m, n), F32),
        compiler_params=_params("parallel", "arbitrary"),
        name="matmul_residual",
    )(a, w, res)


def _ffn_body(x_ref, g_ref, wg_ref, wu_ref, wd_ref, o_ref, h_ref):
    @pl.when(pl.program_id(1) == 0)
    def _():
        x = x_ref[...]
        h_ref[...] = (_rms(x) * g_ref[...]).astype(BF16)
        o_ref[...] = x

    h = h_ref[...]
    gate = _dot(h, wg_ref[...])
    up = _dot(h, wu_ref[...])
    act = (gate * _sigmoid(gate) * up).astype(BF16)
    o_ref[...] += _dot(act, wd_ref[...])


def _ffn(x, gain, w_gate_up, w_down, *, tm, th):
    m, d = x.shape
    hidden = w_down.shape[0]
    nh = hidden // th
    return pl.pallas_call(
        _ffn_body,
        grid=(m // tm, nh),
        in_specs=[pl.BlockSpec((tm, d), lambda i, j: (i, 0)),
                  pl.BlockSpec((1, d), lambda i, j: (0, 0)),
                  pl.BlockSpec((d, th), lambda i, j: (0, j)),
                  pl.BlockSpec((d, th), lambda i, j: (0, nh + j)),
                  pl.BlockSpec((th, d), lambda i, j: (j, 0))],
        out_specs=pl.BlockSpec((tm, d), lambda i, j: (i, 0)),
        out_shape=jax.ShapeDtypeStruct((m, d), F32),
        scratch_shapes=[pltpu.VMEM((tm, d), BF16)],
        compiler_params=_params("parallel", "arbitrary"),
        name="ffn",
    )(x, gain.reshape(1, d), w_gate_up, w_gate_up, w_down)


def _sb_attn_body(q_ref, k_ref, v_ref, o_ref, *, tile):
    qi = pl.program_id(2)
    q = q_ref[...]
    row = lax.broadcasted_iota(jnp.int32, (tile, tile), 0)
    col = lax.broadcasted_iota(jnp.int32, (tile, tile), 1)
    suffix_ones = (row >= col).astype(BF16)
    past = col < row

    def sweep(start, stay, acc, diagonal):
        k = k_ref[pl.ds(start, tile), :]
        v = v_ref[pl.ds(start, tile), :]
        z = _dot_nt(q, k)
        sp = _softplus(z)
        if diagonal:
            sp = jnp.where(past, sp, 0.0)
        hi = sp.astype(BF16)
        lo = (sp - hi.astype(F32)).astype(BF16)
        cum = _dot(hi, suffix_ones) + _dot(lo, suffix_ones)
        p = jnp.exp(z - cum - stay)
        if diagonal:
            p = jnp.where(past, p, 0.0)
        acc = acc + _dot(p.astype(BF16), v)
        return stay + cum[:, 0:1], acc

    stay = jnp.zeros((tile, 1), F32)
    acc = jnp.zeros((tile, HEAD_DIM), F32)
    stay, acc = sweep(pl.multiple_of(qi * tile, tile), stay, acc, True)

    def body(i, carry):
        start = pl.multiple_of((qi - 1 - i) * tile, tile)
        return sweep(start, carry[0], carry[1], False)

    stay, acc = lax.fori_loop(0, qi, body, (stay, acc))
    o_ref[...] = acc.astype(o_ref.dtype)


def _sb_attention(qkv, *, batch, seq, tile):
    m = qkv.shape[0]
    nq = seq // tile
    return pl.pallas_call(
        functools.partial(_sb_attn_body, tile=tile),
        grid=(batch, N_HEADS, nq),
        in_specs=[pl.BlockSpec((tile, HEAD_DIM), lambda b, h, i: (b * nq + i, h)),
                  pl.BlockSpec((seq, HEAD_DIM), lambda b, h, i: (b, N_HEADS + h)),
                  pl.BlockSpec((seq, HEAD_DIM), lambda b, h, i: (b, 2 * N_HEADS + h))],
        out_specs=pl.BlockSpec((tile, HEAD_DIM), lambda b, h, i: (b * nq + i, h)),
        out_shape=jax.ShapeDtypeStruct((m, N_HEADS * HEAD_DIM), BF16),
        compiler_params=_params("parallel", "parallel", "arbitrary"),
        name="sb_attention",
    )(qkv, qkv, qkv)


def _rope(x, cos_t, sin_a, sin_b):
    return x * cos_t + pltpu.roll(x, 96, 1) * sin_a + pltpu.roll(x, 32, 1) * sin_b


def _rms_rope_part(x, gain):
    ms = jnp.sum(x * x, axis=-1, keepdims=True) * (1.0 / MLA_ROPE)
    return x * lax.rsqrt(ms + NORM_EPS) * gain


def _mla_prep_body(d_ref, wq_ref, wkv_ref, qa_ref, kva_ref, qn_ref, qr_ref, kn_ref, kr_ref,
                   cos_ref, sa_ref, sb_ref, q_out, kn_out, v_out, kr_out):
    d = d_ref[...]
    cos_t, sin_a, sin_b = cos_ref[...], sa_ref[...], sb_ref[...]
    scale = MLA_QK ** -0.5
    cq = (_rms(d[:, :MLA_Q_RANK]) * qa_ref[...]).astype(BF16)
    ckv = (_rms(d[:, MLA_Q_RANK:MLA_Q_RANK + MLA_KV_RANK]) * kva_ref[...]).astype(BF16)
    qp = _dot(cq, wq_ref[...])
    kv = _dot(ckv, wkv_ref[...])
    for h in range(N_HEADS):
        c0 = h * MLA_QPAD
        nope = _rms(qp[:, c0:c0 + HEAD_DIM]) * qn_ref[...]
        q_out[:, c0:c0 + HEAD_DIM] = (nope * scale).astype(BF16)
        rot = _rope(_rms_rope_part(qp[:, c0 + HEAD_DIM:c0 + MLA_QPAD], qr_ref[...]), cos_t, sin_a, sin_b)
        q_out[:, c0 + HEAD_DIM:c0 + MLA_QPAD] = (rot * scale).astype(BF16)
        ks = slice(h * HEAD_DIM, (h + 1) * HEAD_DIM)
        kn_out[:, ks] = (_rms(kv[:, ks]) * kn_ref[...]).astype(BF16)
    v_out[...] = kv[:, N_HEADS * HEAD_DIM:].astype(BF16)
    k_rope = d[:, MLA_Q_RANK + MLA_KV_RANK:]
    kr_out[...] = _rope(_rms_rope_part(k_rope, kr_ref[...]), cos_t, sin_a, sin_b).astype(BF16)


def _mla_prep(down, wq, wkv, gains, tables, *, seq, tm):
    m = down.shape[0]
    nt = seq // tm
    full = lambda a: pl.BlockSpec(a.shape, lambda i: (0, 0))
    tab = pl.BlockSpec((tm, LANES), lambda i: (i % nt, 0))
    row = lambda n: pl.BlockSpec((tm, n), lambda i: (i, 0))
    return pl.pallas_call(
        _mla_prep_body,
        grid=(m // tm,),
        in_specs=[row(down.shape[1]), full(wq), full(wkv)] + [full(g) for g in gains] + [tab, tab, tab],
        out_specs=[row(N_HEADS * MLA_QPAD), row(N_HEADS * HEAD_DIM), row(N_HEADS * HEAD_DIM), row(LANES)],
        out_shape=[jax.ShapeDtypeStruct((m, N_HEADS * MLA_QPAD), BF16),
                   jax.ShapeDtypeStruct((m, N_HEADS * HEAD_DIM), BF16),
                   jax.ShapeDtypeStruct((m, N_HEADS * HEAD_DIM), BF16),
                   jax.ShapeDtypeStruct((m, LANES), BF16)],
        compiler_params=_params("parallel"),
        name="mla_prep",
    )(down, wq, wkv, *gains, *tables)


def _mla_attn_body(q_ref, kn_ref, kr_ref, v_ref, o_ref, *, tile):
    qi = pl.program_id(2)
    q = q_ref[...]
    row = lax.broadcasted_iota(jnp.int32, (tile, tile), 0)
    col = lax.broadcasted_iota(jnp.int32, (tile, tile), 1)
    visible = col <= row

    def sweep(start, m_run, l_run, acc, diagonal):
        k = jnp.concatenate([kn_ref[pl.ds(start, tile), :], kr_ref[pl.ds(start, tile), :]], axis=1)
        v = v_ref[pl.ds(start, tile), :]
        s = _dot_nt(q, k)
        if diagonal:
            s = jnp.where(visible, s, -1e30)
        m_new = jnp.maximum(m_run, jnp.max(s, axis=-1, keepdims=True))
        alpha = jnp.exp(m_run - m_new)
        p = jnp.exp(s - m_new)
        l_new = alpha * l_run + jnp.sum(p, axis=-1, keepdims=True)
        acc = alpha * acc + _dot(p.astype(BF16), v)
        return m_new, l_new, acc

    def body(i, carry):
        return sweep(pl.multiple_of(i * tile, tile), *carry, False)

    init = (jnp.full((tile, 1), -1e30, F32), jnp.zeros((tile, 1), F32), jnp.zeros((tile, HEAD_DIM), F32))
    carry = lax.fori_loop(0, qi, body, init)
    _, l_run, acc = sweep(pl.multiple_of(qi * tile, tile), *carry, True)
    o_ref[...] = (acc / l_run).astype(o_ref.dtype)


def _mla_attention(q, kn, kr, v, *, batch, seq, tile):
    m = q.shape[0]
    nq = seq // tile
    return pl.pallas_call(
        functools.partial(_mla_attn_body, tile=tile),
        grid=(batch, N_HEADS, nq),
        in_specs=[pl.BlockSpec((tile, MLA_QPAD), lambda b, h, i: (b * nq + i, h)),
                  pl.BlockSpec((seq, HEAD_DIM), lambda b, h, i: (b, h)),
                  pl.BlockSpec((seq, LANES), lambda b, h, i: (b, 0)),
                  pl.BlockSpec((seq, HEAD_DIM), lambda b, h, i: (b, h))],
        out_specs=pl.BlockSpec((tile, HEAD_DIM), lambda b, h, i: (b * nq + i, h)),
        out_shape=jax.ShapeDtypeStruct((m, N_HEADS * HEAD_DIM), BF16),
        compiler_params=_params("parallel", "parallel", "arbitrary"),
        name="mla_attention",
    )(q, kn, kr, v)


def _dn_body(x_ref, ab_ref, cw_ref, alog_ref, dtb_ref, onorm_ref, o_ref, state_ref, ext_ref):
    c = DN_CHUNK
    width = N_HEADS * HEAD_DIM

    @pl.when(pl.program_id(1) == 0)
    def _():
        state_ref[...] = jnp.zeros_like(state_ref)
        ext_ref[0:8, :] = jnp.zeros((8, 3 * width), F32)

    ext_ref[8:8 + c, :] = x_ref[:, 0:3 * width].astype(F32)

    def conv_silu(col):
        sl = slice(col, col + HEAD_DIM)
        y = cw_ref[DN_CONV - 1:DN_CONV, sl] * ext_ref[8:8 + c, sl]
        for j in range(DN_CONV - 1):
            y = y + cw_ref[j:j + 1, sl] * ext_ref[5 + j:5 + j + c, sl]
        return y * _sigmoid(y)

    ab = ab_ref[...]
    g_all = -jnp.exp(alog_ref[...]) * _softplus(ab + dtb_ref[...])
    beta_all = _sigmoid(ab)
    row = lax.broadcasted_iota(jnp.int32, (c, c), 0)
    col = lax.broadcasted_iota(jnp.int32, (c, c), 1)
    causal = row >= col
    strict = row > col
    eye = (row == col).astype(F32)
    g_parts = _split3(g_all)
    prefix_ones = causal.astype(BF16)
    all_ones = jnp.ones((c, c), BF16)
    gc_all = sum(_dot(prefix_ones, p) for p in g_parts)
    gt_all = sum(_dot(all_ones, p) for p in g_parts)
    gc_rows = gc_all.T
    levels = []
    b = 1
    while b < c:
        shift = b.bit_length() - 1
        levels.append(((row >> (shift + 1)) == (col >> (shift + 1)))
                      & (((row >> shift) & 1) == 1) & (((col >> shift) & 1) == 0))
        b *= 2

    for h in range(N_HEADS):
        q = conv_silu(h * HEAD_DIM)
        k = conv_silu(width + h * HEAD_DIM)
        v = conv_silu(2 * width + h * HEAD_DIM)
        q = q * lax.rsqrt(jnp.sum(q * q, axis=-1, keepdims=True) + NORM_EPS) * (HEAD_DIM ** -0.5)
        k = k * lax.rsqrt(jnp.sum(k * k, axis=-1, keepdims=True) + NORM_EPS)
        g_col = gc_all[:, h:h + 1]
        g_row = gc_rows[h:h + 1, :]
        g_tot = gt_all[:, h:h + 1]
        beta = beta_all[:, N_HEADS + h:N_HEADS + h + 1]
        decay = jnp.where(causal, jnp.exp(jnp.where(causal, g_col - g_row, 0.0)), 0.0)
        k_bf = k.astype(BF16)
        k_beta = k * beta
        lower = jnp.where(strict, _dot_nt(k_beta.astype(BF16), k_bf) * decay, 0.0)
        inv = eye - jnp.where(levels[0], lower, 0.0)
        for mask in levels[1:]:
            step = _dot(inv.astype(BF16), jnp.where(mask, lower, 0.0).astype(BF16))
            inv = inv - _dot(step.astype(BF16), inv.astype(BF16))
        exp_g = jnp.exp(g_col)
        rhs = jnp.concatenate([v * beta, k_beta * exp_g], axis=1).astype(BF16)
        uw = _dot(inv.astype(BF16), rhs)
        u, w = uw[:, :HEAD_DIM], uw[:, HEAD_DIM:]
        attn = jnp.where(causal, _dot_nt(q.astype(BF16), k_bf) * decay, 0.0)
        q_dec = q * exp_g
        k_dec = k * jnp.exp(g_tot - g_col)
        state = state_ref[h]
        ws_qs = _dot(jnp.concatenate([w, q_dec], axis=0).astype(BF16), state.astype(BF16))
        v_new = (u - ws_qs[:c]).astype(BF16)
        out = ws_qs[c:] + _dot(attn.astype(BF16), v_new)
        state_ref[h] = state * jnp.exp(g_tot) + _dot(k_dec.T.astype(BF16), v_new)
        z = x_ref[:, 3 * width + h * HEAD_DIM:3 * width + (h + 1) * HEAD_DIM].astype(F32)
        o_ref[:, h * HEAD_DIM:(h + 1) * HEAD_DIM] = (
            _rms(out) * onorm_ref[...] * (z * _sigmoid(z))).astype(o_ref.dtype)

    ext_ref[0:8, :] = ext_ref[c:c + 8, :]


def _dn_core(qkvz, ab, conv_w, a_log, dt_bias, out_norm, *, batch, seq):
    m = qkvz.shape[0]
    nc = seq // DN_CHUNK
    width = N_HEADS * HEAD_DIM
    full = lambda a: pl.BlockSpec(a.shape, lambda b, t: (0, 0))
    return pl.pallas_call(
        _dn_body,
        grid=(batch, nc),
        in_specs=[pl.BlockSpec((DN_CHUNK, 4 * width), lambda b, t: (b * nc + t, 0)),
                  pl.BlockSpec((DN_CHUNK, LANES), lambda b, t: (b * nc + t, 0)),
                  full(conv_w), full(a_log), full(dt_bias), full(out_norm)],
        out_specs=pl.BlockSpec((DN_CHUNK, width), lambda b, t: (b * nc + t, 0)),
        out_shape=jax.ShapeDtypeStruct((m, width), BF16),
        scratch_shapes=[pltpu.VMEM((N_HEADS, HEAD_DIM, HEAD_DIM), F32),
                        pltpu.VMEM((DN_CHUNK + 8, 3 * width), F32)],
        compiler_params=_params("parallel", "arbitrary"),
        name="dn_core",
    )(qkvz, ab, conv_w, a_log, dt_bias, out_norm)


def _pad_lanes(vec, n=LANES):
    return jnp.pad(vec.astype(F32), (0, n - vec.shape[0])).reshape(1, n)


def _deltanet_mixer(x, mix_norm, w_in, conv_w, a_log, dt_bias, out_norm, w_out, *, batch, seq):
    width = N_HEADS * HEAD_DIM
    w_main = w_in[:, :4 * width].astype(BF16)
    w_gate = jnp.pad(w_in[:, 4 * width:], ((0, 0), (0, LANES - 2 * N_HEADS))).astype(BF16)
    qkvz = _norm_matmul(x, mix_norm, w_main, tm=1024, tn=1024, out_dtype=BF16)
    ab = _norm_matmul(x, mix_norm, w_gate, tm=1024, tn=LANES, out_dtype=F32)
    o = _dn_core(qkvz, ab, conv_w, _pad_lanes(a_log), _pad_lanes(dt_bias), out_norm.reshape(1, HEAD_DIM),
                 batch=batch, seq=seq)
    return _matmul_residual(o, w_out.astype(BF16), x, tm=1024, tn=1024)


def _stick_breaking_mixer(x, mix_norm, w_qkv, q_norm, k_norm, w_out, *, batch, seq):
    qkv = _norm_matmul(x, mix_norm, w_qkv.astype(BF16), tm=1024, tn=N_HEADS * HEAD_DIM, out_dtype=BF16,
                       epilogue=_sb_epilogue,
                       extra=(q_norm.reshape(1, HEAD_DIM), k_norm.reshape(1, HEAD_DIM)))
    o = _sb_attention(qkv, batch=batch, seq=seq, tile=256)
    return _matmul_residual(o, w_out.astype(BF16), x, tm=1024, tn=1024)


def _rope_tables(seq):
    half = MLA_ROPE // 2
    inv_freq = ROPE_THETA ** (-jnp.arange(0, MLA_ROPE, 2, dtype=F32) / MLA_ROPE)
    ang = jnp.arange(seq, dtype=F32)[:, None] * inv_freq[None, :]
    cos, sin, zero = jnp.cos(ang), jnp.sin(ang), jnp.zeros((seq, half), F32)
    pad = jnp.zeros((seq, LANES - MLA_ROPE), F32)
    return (jnp.concatenate([cos, cos, pad], axis=1),
            jnp.concatenate([-sin, zero, pad], axis=1),
            jnp.concatenate([zero, sin, pad], axis=1))


def _mla_mixer(x, mix_norm, w_down, q_a_norm, kv_a_norm, w_uq, w_ukv, q_nope_norm, q_rope_norm,
               k_nope_norm, k_rope_norm, w_out, *, batch, seq):
    down_w = jnp.pad(w_down, ((0, 0), (0, 512 - w_down.shape[1]))).astype(BF16)
    wq = jnp.pad(w_uq.reshape(MLA_Q_RANK, N_HEADS, MLA_QK),
                 ((0, 0), (0, 0), (0, MLA_QPAD - MLA_QK))).reshape(MLA_Q_RANK, N_HEADS * MLA_QPAD).astype(BF16)
    wkv = w_ukv.reshape(MLA_KV_RANK, N_HEADS, 2, HEAD_DIM).transpose(0, 2, 1, 3).reshape(
        MLA_KV_RANK, 2 * N_HEADS * HEAD_DIM).astype(BF16)
    down = _norm_matmul(x, mix_norm, down_w, tm=1024, tn=512, out_dtype=F32)
    gains = (q_a_norm.reshape(1, -1), kv_a_norm.reshape(1, -1), q_nope_norm.reshape(1, -1),
             _pad_lanes(q_rope_norm), k_nope_norm.reshape(1, -1), _pad_lanes(k_rope_norm))
    q, kn, v, kr = _mla_prep(down, wq, wkv, gains, _rope_tables(seq), seq=seq, tm=512)
    o = _mla_attention(q, kn, kr, v, batch=batch, seq=seq, tile=256)
    return _matmul_residual(o, w_out.astype(BF16), x, tm=1024, tn=1024)


def kernel(x, l0_mix_norm, l0_dn_w_in, l0_dn_conv_w, l0_dn_a_log, l0_dn_dt_bias, l0_dn_out_norm, l0_dn_w_out, l0_ffn_norm, l0_ffn_w_gate_up, l0_ffn_w_down, l1_mix_norm, l1_sb_w_qkv, l1_sb_q_norm, l1_sb_k_norm, l1_sb_w_out, l1_ffn_norm, l1_ffn_w_gate_up, l1_ffn_w_down, l2_mix_norm, l2_mla_w_down, l2_mla_q_a_norm, l2_mla_kv_a_norm, l2_mla_w_uq, l2_mla_w_ukv, l2_mla_q_nope_norm, l2_mla_q_rope_norm, l2_mla_k_nope_norm, l2_mla_k_rope_norm, l2_mla_w_out, l2_ffn_norm, l2_ffn_w_gate_up, l2_ffn_w_down, l3_mix_norm, l3_dn_w_in, l3_dn_conv_w, l3_dn_a_log, l3_dn_dt_bias, l3_dn_out_norm, l3_dn_w_out, l3_ffn_norm, l3_ffn_w_gate_up, l3_ffn_w_down):
    batch, seq, d_model = x.shape
    shape = dict(batch=batch, seq=seq)
    ffn = lambda y, g, wgu, wd: _ffn(y, g, wgu.astype(BF16), wd.astype(BF16), tm=512, th=1408)
    y = x.reshape(batch * seq, d_model)
    y = _deltanet_mixer(y, l0_mix_norm, l0_dn_w_in, l0_dn_conv_w, l0_dn_a_log, l0_dn_dt_bias, l0_dn_out_norm,
                        l0_dn_w_out, **shape)
    y = ffn(y, l0_ffn_norm, l0_ffn_w_gate_up, l0_ffn_w_down)
    y = _stick_breaking_mixer(y, l1_mix_norm, l1_sb_w_qkv, l1_sb_q_norm, l1_sb_k_norm, l1_sb_w_out, **shape)
    y = ffn(y, l1_ffn_norm, l1_ffn_w_gate_up, l1_ffn_w_down)
    y = _mla_mixer(y, l2_mix_norm, l2_mla_w_down, l2_mla_q_a_norm, l2_mla_kv_a_norm, l2_mla_w_uq, l2_mla_w_ukv,
                   l2_mla_q_nope_norm, l2_mla_q_rope_norm, l2_mla_k_nope_norm, l2_mla_k_rope_norm, l2_mla_w_out,
                   **shape)
    y = ffn(y, l2_ffn_norm, l2_ffn_w_gate_up, l2_ffn_w_down)
    y = _deltanet_mixer(y, l3_mix_norm, l3_dn_w_in, l3_dn_conv_w, l3_dn_a_log, l3_dn_dt_bias, l3_dn_out_norm,
                        l3_dn_w_out, **shape)
    y = ffn(y, l3_ffn_norm, l3_ffn_w_gate_up, l3_ffn_w_down)
    return y.reshape(batch, seq, d_model)
```

```python
import functools

import jax
import jax.numpy as jnp
from jax import lax
from jax.experimental import pallas as pl
from jax.experimental.pallas import tpu as pltpu

F32 = jnp.float32
BF16 = jnp.bfloat16

NORM_EPS = 1e-6
LANES = 128
HEAD_DIM = 128
N_HEADS = 8
DN_CONV = 4
DN_CHUNK = 128
MLA_ROPE = 64
MLA_QK = 192
MLA_Q_RANK = 256
MLA_KV_RANK = 128
MLA_QPAD = 256
ROPE_THETA = 10000.0
VMEM_LIMIT = 56 * 1024 * 1024


def _params(*sem):
    return pltpu.CompilerParams(dimension_semantics=sem, vmem_limit_bytes=VMEM_LIMIT)


def _rms(x):
    return x * lax.rsqrt(jnp.mean(x * x, axis=-1, keepdims=True) + NORM_EPS)


def _sigmoid(x):
    return 1.0 / (1.0 + jnp.exp(-x))


def _softplus(x):
    return jnp.maximum(x, 0.0) + jnp.log(1.0 + jnp.exp(-jnp.abs(x)))


def _dot(a, b):
    return jnp.dot(a, b, preferred_element_type=F32)


def _dot_nt(a, b):
    return lax.dot_general(a, b, (((1,), (1,)), ((), ())), preferred_element_type=F32)


def _split3(x):
    hi = x.astype(BF16)
    r = x - hi.astype(F32)
    mid = r.astype(BF16)
    lo = (r - mid.astype(F32)).astype(BF16)
    return hi, mid, lo


def _plain_epilogue(acc, j, extra, o_ref):
    del j, extra
    o_ref[...] = acc.astype(o_ref.dtype)


def _sb_epilogue(acc, j, extra, o_ref):
    qg_ref, kg_ref = extra

    def head_norm(gain, scale):
        for h in range(acc.shape[1] // HEAD_DIM):
            sl = slice(h * HEAD_DIM, (h + 1) * HEAD_DIM)
            o_ref[:, sl] = (_rms(acc[:, sl]) * gain * scale).astype(o_ref.dtype)

    @pl.when(j == 0)
    def _():
        head_norm(qg_ref[...], HEAD_DIM ** -0.5)

    @pl.when(j == 1)
    def _():
        head_norm(kg_ref[...], 1.0)

    @pl.when(j == 2)
    def _():
        o_ref[...] = acc.astype(o_ref.dtype)


def _norm_matmul_body(x_ref, g_ref, w_ref, *rest, epilogue, n_extra):
    extra, o_ref, h_ref = rest[:n_extra], rest[n_extra], rest[n_extra + 1]
    j = pl.program_id(1)

    @pl.when(j == 0)
    def _():
        h_ref[...] = (_rms(x_ref[...]) * g_ref[...]).astype(BF16)

    epilogue(_dot(h_ref[...], w_ref[...]), j, extra, o_ref)


def _norm_matmul(x, gain, w, *, tm, tn, out_dtype, epilogue=_plain_epilogue, extra=()):
    m, d = x.shape
    n = w.shape[1]
    tm = min(tm, m)
    extra_specs = [pl.BlockSpec(e.shape, lambda i, j: (0, 0)) for e in extra]
    return pl.pallas_call(
        functools.partial(_norm_matmul_body, epilogue=epilogue, n_extra=len(extra)),
        grid=(m // tm, n // tn),
        in_specs=[pl.BlockSpec((tm, d), lambda i, j: (i, 0)),
                  pl.BlockSpec((1, d), lambda i, j: (0, 0)),
                  pl.BlockSpec((d, tn), lambda i, j: (0, j))] + extra_specs,
        out_specs=pl.BlockSpec((tm, tn), lambda i, j: (i, j)),
        out_shape=jax.ShapeDtypeStruct((m, n), out_dtype),
        scratch_shapes=[pltpu.VMEM((tm, d), BF16)],
        compiler_params=_params("parallel", "arbitrary"),
        name="norm_matmul",
    )(x, gain.reshape(1, d), w, *extra)


def _matmul_res_body(a_ref, w_ref, r_ref, o_ref):
    o_ref[...] = r_ref[...] + _dot(a_ref[...], w_ref[...])


def _matmul_residual(a, w, res, *, tm, tn):
    m, k = a.shape
    n = w.shape[1]
    tm = min(tm, m)
    return pl.pallas_call(
        _matmul_res_body,
        grid=(m // tm, n // tn),
        in_specs=[pl.BlockSpec((tm, k), lambda i, j: (i, 0)),
                  pl.BlockSpec((k, tn), lambda i, j: (0, j)),
                  pl.BlockSpec((tm, tn), lambda i, j: (i, j))],
        out_specs=pl.BlockSpec((tm, tn), lambda i, j: (i, j)),
        out_shape=jax.ShapeDtypeStruct((m, n), F32),
        compiler_params=_params("parallel", "arbitrary"),
        name="matmul_residual",
    )(a, w, res)


def _ffn_body(x_ref, g_ref, wg_ref, wu_ref, wd_ref, o_ref, h_ref):
    @pl.when(pl.program_id(1) == 0)
    def _():
        x = x_ref[...]
        h_ref[...] = (_rms(x) * g_ref[...]).astype(BF16)
        o_ref[...] = x

    h = h_ref[...]
    gate = _dot(h, wg_ref[...])
    up = _dot(h, wu_ref[...])
    act = (gate * _sigmoid(gate) * up).astype(BF16)
    o_ref[...] += _dot(act, wd_ref[...])


def _ffn(x, gain, w_gate_up, w_down, *, tm, th):
    m, d = x.shape
    hidden = w_down.shape[0]
    nh = hidden // th
    return pl.pallas_call(
        _ffn_body,
        grid=(m // tm, nh),
        in_specs=[pl.BlockSpec((tm, d), lambda i, j: (i, 0)),
                  pl.BlockSpec((1, d), lambda i, j: (0, 0)),
                  pl.BlockSpec((d, th), lambda i, j: (0, j)),
                  pl.BlockSpec((d, th), lambda i, j: (0, nh + j)),
                  pl.BlockSpec((th, d), lambda i, j: (j, 0))],
        out_specs=pl.BlockSpec((tm, d), lambda i, j: (i, 0)),
        out_shape=jax.ShapeDtypeStruct((m, d), F32),
        scratch_shapes=[pltpu.VMEM((tm, d), BF16)],
        compiler_params=_params("parallel", "arbitrary"),
        name="ffn",
    )(x, gain.reshape(1, d), w_gate_up, w_gate_up, w_down)


def _sb_attn_body(q_ref, k_ref, v_ref, o_ref, *, tile, nsub):
    qi = pl.program_id(2)
    qs = [q_ref[s * tile:(s + 1) * tile, :] for s in range(nsub)]
    row = lax.broadcasted_iota(jnp.int32, (tile, tile), 0)
    col = lax.broadcasted_iota(jnp.int32, (tile, tile), 1)
    suffix_ones = (row >= col).astype(BF16)
    past = col < row

    def sweep(start, subs, stays, accs, diag_sub):
        k = k_ref[pl.ds(start, tile), :]
        v = v_ref[pl.ds(start, tile), :]
        zs = [_dot_nt(qs[s], k) for s in subs]
        sps = [_softplus(z) for z in zs]
        sps = [jnp.where(past, sp, 0.0) if s == diag_sub else sp for s, sp in zip(subs, sps)]
        his = [sp.astype(BF16) for sp in sps]
        los = [(sp - hi.astype(F32)).astype(BF16) for sp, hi in zip(sps, his)]
        cums = [_dot(hi, suffix_ones) + _dot(lo, suffix_ones) for hi, lo in zip(his, los)]
        ps = [jnp.exp(z - cum - stays[s]) for s, z, cum in zip(subs, zs, cums)]
        ps = [jnp.where(past, p, 0.0) if s == diag_sub else p for s, p in zip(subs, ps)]
        stays, accs = list(stays), list(accs)
        for s, p, cum in zip(subs, ps, cums):
            accs[s] = accs[s] + _dot(p.astype(BF16), v)
            stays[s] = stays[s] + cum[:, 0:1]
        return stays, accs

    stays = [jnp.zeros((tile, 1), F32)] * nsub
    accs = [jnp.zeros((tile, HEAD_DIM), F32)] * nsub
    base = qi * nsub
    for j in reversed(range(nsub)):
        start = pl.multiple_of((base + j) * tile, tile)
        stays, accs = sweep(start, list(range(j, nsub)), stays, accs, j)

    def body(i, carry):
        start = pl.multiple_of((base - 1 - i) * tile, tile)
        st, ac = sweep(start, list(range(nsub)), carry[:nsub], carry[nsub:], None)
        return tuple(st) + tuple(ac)

    carry = lax.fori_loop(0, base, body, tuple(stays) + tuple(accs))
    for s in range(nsub):
        o_ref[s * tile:(s + 1) * tile, :] = carry[nsub + s].astype(o_ref.dtype)


def _sb_attention(qkv, *, batch, seq, tile, nsub):
    m = qkv.shape[0]
    nq = seq // (tile * nsub)
    return pl.pallas_call(
        functools.partial(_sb_attn_body, tile=tile, nsub=nsub),
        grid=(batch, N_HEADS, nq),
        in_specs=[pl.BlockSpec((tile * nsub, HEAD_DIM), lambda b, h, i: (b * nq + i, h)),
                  pl.BlockSpec((seq, HEAD_DIM), lambda b, h, i: (b, N_HEADS + h)),
                  pl.BlockSpec((seq, HEAD_DIM), lambda b, h, i: (b, 2 * N_HEADS + h))],
        out_specs=pl.BlockSpec((tile * nsub, HEAD_DIM), lambda b, h, i: (b * nq + i, h)),
        out_shape=jax.ShapeDtypeStruct((m, N_HEADS * HEAD_DIM), BF16),
        compiler_params=_params("parallel", "parallel", "arbitrary"),
        name="sb_attention",
    )(qkv, qkv, qkv)


def _rope(x, cos_t, sin_a, sin_b):
    return x * cos_t + pltpu.roll(x, 96, 1) * sin_a + pltpu.roll(x, 32, 1) * sin_b


def _rms_rope_part(x, gain):
    ms = jnp.sum(x * x, axis=-1, keepdims=True) * (1.0 / MLA_ROPE)
    return x * lax.rsqrt(ms + NORM_EPS) * gain


def _mla_prep_body(d_ref, wq_ref, wkv_ref, qa_ref, kva_ref, qn_ref, qr_ref, kn_ref, kr_ref,
                   cos_ref, sa_ref, sb_ref, q_out, kn_out, v_out, kr_out):
    d = d_ref[...]
    cos_t, sin_a, sin_b = cos_ref[...], sa_ref[...], sb_ref[...]
    scale = MLA_QK ** -0.5
    cq = (_rms(d[:, :MLA_Q_RANK]) * qa_ref[...]).astype(BF16)
    ckv = (_rms(d[:, MLA_Q_RANK:MLA_Q_RANK + MLA_KV_RANK]) * kva_ref[...]).astype(BF16)
    qp = _dot(cq, wq_ref[...])
    kv = _dot(ckv, wkv_ref[...])
    for h in range(N_HEADS):
        c0 = h * MLA_QPAD
        nope = _rms(qp[:, c0:c0 + HEAD_DIM]) * qn_ref[...]
        q_out[:, c0:c0 + HEAD_DIM] = (nope * scale).astype(BF16)
        rot = _rope(_rms_rope_part(qp[:, c0 + HEAD_DIM:c0 + MLA_QPAD], qr_ref[...]), cos_t, sin_a, sin_b)
        q_out[:, c0 + HEAD_DIM:c0 + MLA_QPAD] = (rot * scale).astype(BF16)
        ks = slice(h * HEAD_DIM, (h + 1) * HEAD_DIM)
        kn_out[:, ks] = (_rms(kv[:, ks]) * kn_ref[...]).astype(BF16)
    v_out[...] = kv[:, N_HEADS * HEAD_DIM:].astype(BF16)
    k_rope = d[:, MLA_Q_RANK + MLA_KV_RANK:]
    kr_out[...] = _rope(_rms_rope_part(k_rope, kr_ref[...]), cos_t, sin_a, sin_b).astype(BF16)


def _mla_prep(down, wq, wkv, gains, tables, *, seq, tm):
    m = down.shape[0]
    nt = seq // tm
    full = lambda a: pl.BlockSpec(a.shape, lambda i: (0, 0))
    tab = pl.BlockSpec((tm, LANES), lambda i: (i % nt, 0))
    row = lambda n: pl.BlockSpec((tm, n), lambda i: (i, 0))
    return pl.pallas_call(
        _mla_prep_body,
        grid=(m // tm,),
        in_specs=[row(down.shape[1]), full(wq), full(wkv)] + [full(g) for g in gains] + [tab, tab, tab],
        out_specs=[row(N_HEADS * MLA_QPAD), row(N_HEADS * HEAD_DIM), row(N_HEADS * HEAD_DIM), row(LANES)],
        out_shape=[jax.ShapeDtypeStruct((m, N_HEADS * MLA_QPAD), BF16),
                   jax.ShapeDtypeStruct((m, N_HEADS * HEAD_DIM), BF16),
                   jax.ShapeDtypeStruct((m, N_HEADS * HEAD_DIM), BF16),
                   jax.ShapeDtypeStruct((m, LANES), BF16)],
        compiler_params=_params("parallel"),
        name="mla_prep",
    )(down, wq, wkv, *gains, *tables)


def _mla_attn_body(q_ref, kn_ref, kr_ref, v_ref, o_ref, *, tile, nsub):
    qi = pl.program_id(2)
    qs = [q_ref[s * tile:(s + 1) * tile, :] for s in range(nsub)]
    row = lax.broadcasted_iota(jnp.int32, (tile, tile), 0)
    col = lax.broadcasted_iota(jnp.int32, (tile, tile), 1)
    visible = col <= row

    def sweep(start, subs, carry, diag_sub):
        k = jnp.concatenate([kn_ref[pl.ds(start, tile), :], kr_ref[pl.ds(start, tile), :]], axis=1)
        v = v_ref[pl.ds(start, tile), :]
        ss = [_dot_nt(qs[s], k) for s in subs]
        ss = [jnp.where(visible, x, -1e30) if s == diag_sub else x for s, x in zip(subs, ss)]
        m_new = [jnp.maximum(carry[s][0], jnp.max(x, axis=-1, keepdims=True)) for s, x in zip(subs, ss)]
        ps = [jnp.exp(x - mn) for x, mn in zip(ss, m_new)]
        carry = list(carry)
        for s, p, mn in zip(subs, ps, m_new):
            m_run, l_run, acc = carry[s]
            alpha = jnp.exp(m_run - mn)
            carry[s] = (mn, alpha * l_run + jnp.sum(p, axis=-1, keepdims=True),
                        alpha * acc + _dot(p.astype(BF16), v))
        return carry

    def flat(carry):
        return tuple(x for c in carry for x in c)

    def body(i, fc):
        carry = [tuple(fc[3 * s:3 * s + 3]) for s in range(nsub)]
        return flat(sweep(pl.multiple_of(i * tile, tile), list(range(nsub)), carry, None))

    base = qi * nsub
    init = [(jnp.full((tile, 1), -1e30, F32), jnp.zeros((tile, 1), F32),
             jnp.zeros((tile, HEAD_DIM), F32))] * nsub
    fc = lax.fori_loop(0, base, body, flat(init))
    carry = [tuple(fc[3 * s:3 * s + 3]) for s in range(nsub)]
    for j in range(nsub):
        start = pl.multiple_of((base + j) * tile, tile)
        carry = sweep(start, list(range(j, nsub)), carry, j)
    for s in range(nsub):
        o_ref[s * tile:(s + 1) * tile, :] = (carry[s][2] / carry[s][1]).astype(o_ref.dtype)


def _mla_attention(q, kn, kr, v, *, batch, seq, tile, nsub):
    m = q.shape[0]
    nq = seq // (tile * nsub)
    return pl.pallas_call(
        functools.partial(_mla_attn_body, tile=tile, nsub=nsub),
        grid=(batch, N_HEADS, nq),
        in_specs=[pl.BlockSpec((tile * nsub, MLA_QPAD), lambda b, h, i: (b * nq + i, h)),
                  pl.BlockSpec((seq, HEAD_DIM), lambda b, h, i: (b, h)),
                  pl.BlockSpec((seq, LANES), lambda b, h, i: (b, 0)),
                  pl.BlockSpec((seq, HEAD_DIM), lambda b, h, i: (b, h))],
        out_specs=pl.BlockSpec((tile * nsub, HEAD_DIM), lambda b, h, i: (b * nq + i, h)),
        out_shape=jax.ShapeDtypeStruct((m, N_HEADS * HEAD_DIM), BF16),
        compiler_params=_params("parallel", "parallel", "arbitrary"),
        name="mla_attention",
    )(q, kn, kr, v)


def _dn_body(x_ref, ab_ref, cw_ref, alog_ref, dtb_ref, onorm_ref, o_ref, state_ref, ext_ref):
    c = DN_CHUNK
    width = N_HEADS * HEAD_DIM

    @pl.when(pl.program_id(1) == 0)
    def _():
        state_ref[...] = jnp.zeros_like(state_ref)
        ext_ref[0:8, :] = jnp.zeros((8, 3 * width), F32)

    ext_ref[8:8 + c, :] = x_ref[:, 0:3 * width].astype(F32)

    def conv_silu(col):
        sl = slice(col, col + HEAD_DIM)
        y = cw_ref[DN_CONV - 1:DN_CONV, sl] * ext_ref[8:8 + c, sl]
        for j in range(DN_CONV - 1):
            y = y + cw_ref[j:j + 1, sl] * ext_ref[5 + j:5 + j + c, sl]
        return y * _sigmoid(y)

    ab = ab_ref[...]
    g_all = -jnp.exp(alog_ref[...]) * _softplus(ab + dtb_ref[...])
    beta_all = _sigmoid(ab)
    row = lax.broadcasted_iota(jnp.int32, (c, c), 0)
    col = lax.broadcasted_iota(jnp.int32, (c, c), 1)
    causal = row >= col
    strict = row > col
    eye = (row == col).astype(F32)
    g_parts = _split3(g_all)
    prefix_ones = causal.astype(BF16)
    all_ones = jnp.ones((c, c), BF16)
    gc_all = sum(_dot(prefix_ones, p) for p in g_parts)
    gt_all = sum(_dot(all_ones, p) for p in g_parts)
    gc_rows = gc_all.T
    levels = []
    b = 1
    while b < c:
        shift = b.bit_length() - 1
        levels.append(((row >> (shift + 1)) == (col >> (shift + 1)))
                      & (((row >> shift) & 1) == 1) & (((col >> shift) & 1) == 0))
        b *= 2

    heads = range(N_HEADS)
    q, k, v = [], [], []
    for h in heads:
        qh = conv_silu(h * HEAD_DIM)
        kh = conv_silu(width + h * HEAD_DIM)
        q.append(qh * lax.rsqrt(jnp.sum(qh * qh, axis=-1, keepdims=True) + NORM_EPS) * (HEAD_DIM ** -0.5))
        k.append(kh * lax.rsqrt(jnp.sum(kh * kh, axis=-1, keepdims=True) + NORM_EPS))
        v.append(conv_silu(2 * width + h * HEAD_DIM))
    g_col = [gc_all[:, h:h + 1] for h in heads]
    g_tot = [gt_all[:, h:h + 1] for h in heads]
    beta = [beta_all[:, N_HEADS + h:N_HEADS + h + 1] for h in heads]
    decay = [jnp.where(causal, jnp.exp(jnp.where(causal, g_col[h] - gc_rows[h:h + 1, :], 0.0)), 0.0)
             for h in heads]
    k_bf = [k[h].astype(BF16) for h in heads]
    k_beta = [k[h] * beta[h] for h in heads]
    kk = [_dot_nt(k_beta[h].astype(BF16), k_bf[h]) for h in heads]
    qk = [_dot_nt(q[h].astype(BF16), k_bf[h]) for h in heads]
    lower = [jnp.where(strict, kk[h] * decay[h], 0.0) for h in heads]
    inv = [eye - jnp.where(levels[0], lower[h], 0.0) for h in heads]
    for mask in levels[1:]:
        inv_bf = [inv[h].astype(BF16) for h in heads]
        step = [_dot(inv_bf[h], jnp.where(mask, lower[h], 0.0).astype(BF16)) for h in heads]
        inv = [inv[h] - _dot(step[h].astype(BF16), inv_bf[h]) for h in heads]
    exp_g = [jnp.exp(g_col[h]) for h in heads]
    uw = [_dot(inv[h].astype(BF16),
               jnp.concatenate([v[h] * beta[h], k_beta[h] * exp_g[h]], axis=1).astype(BF16)) for h in heads]
    state = [state_ref[h] for h in heads]
    ws_qs = [_dot(jnp.concatenate([uw[h][:, HEAD_DIM:], q[h] * exp_g[h]], axis=0).astype(BF16),
                  state[h].astype(BF16)) for h in heads]
    v_new = [(uw[h][:, :HEAD_DIM] - ws_qs[h][:c]).astype(BF16) for h in heads]
    attn = [jnp.where(causal, qk[h] * decay[h], 0.0).astype(BF16) for h in heads]
    out = [ws_qs[h][c:] + _dot(attn[h], v_new[h]) for h in heads]
    for h in heads:
        k_dec = k[h] * jnp.exp(g_tot[h] - g_col[h])
        state_ref[h] = state[h] * jnp.exp(g_tot[h]) + _dot(k_dec.T.astype(BF16), v_new[h])
    for h in heads:
        z = x_ref[:, 3 * width + h * HEAD_DIM:3 * width + (h + 1) * HEAD_DIM].astype(F32)
        o_ref[:, h * HEAD_DIM:(h + 1) * HEAD_DIM] = (
            _rms(out[h]) * onorm_ref[...] * (z * _sigmoid(z))).astype(o_ref.dtype)

    ext_ref[0:8, :] = ext_ref[c:c + 8, :]


def _dn_core(qkvz, ab, conv_w, a_log, dt_bias, out_norm, *, batch, seq):
    m = qkvz.shape[0]
    nc = seq // DN_CHUNK
    width = N_HEADS * HEAD_DIM
    full = lambda a: pl.BlockSpec(a.shape, lambda b, t: (0, 0))
    return pl.pallas_call(
        _dn_body,
        grid=(batch, nc),
        in_specs=[pl.BlockSpec((DN_CHUNK, 4 * width), lambda b, t: (b * nc + t, 0)),
                  pl.BlockSpec((DN_CHUNK, LANES), lambda b, t: (b * nc + t, 0)),
                  full(conv_w), full(a_log), full(dt_bias), full(out_norm)],
        out_specs=pl.BlockSpec((DN_CHUNK, width), lambda b, t: (b * nc + t, 0)),
        out_shape=jax.ShapeDtypeStruct((m, width), BF16),
        scratch_shapes=[pltpu.VMEM((N_HEADS, HEAD_DIM, HEAD_DIM), F32),
                        pltpu.VMEM((DN_CHUNK + 8, 3 * width), F32)],
        compiler_params=_params("parallel", "arbitrary"),
        name="dn_core",
    )(qkvz, ab, conv_w, a_log, dt_bias, out_norm)


def _pad_lanes(vec, n=LANES):
    return jnp.pad(vec.astype(F32), (0, n - vec.shape[0])).reshape(1, n)


def _deltanet_mixer(x, mix_norm, w_in, conv_w, a_log, dt_bias, out_norm, w_out, *, batch, seq):
    width = N_HEADS * HEAD_DIM
    w_main = w_in[:, :4 * width].astype(BF16)
    w_gate = jnp.pad(w_in[:, 4 * width:], ((0, 0), (0, LANES - 2 * N_HEADS))).astype(BF16)
    qkvz = _norm_matmul(x, mix_norm, w_main, tm=1024, tn=1024, out_dtype=BF16)
    ab = _norm_matmul(x, mix_norm, w_gate, tm=1024, tn=LANES, out_dtype=F32)
    o = _dn_core(qkvz, ab, conv_w, _pad_lanes(a_log), _pad_lanes(dt_bias), out_norm.reshape(1, HEAD_DIM),
                 batch=batch, seq=seq)
    return _matmul_residual(o, w_out.astype(BF16), x, tm=1024, tn=1024)


def _stick_breaking_mixer(x, mix_norm, w_qkv, q_norm, k_norm, w_out, *, batch, seq):
    qkv = _norm_matmul(x, mix_norm, w_qkv.astype(BF16), tm=1024, tn=N_HEADS * HEAD_DIM, out_dtype=BF16,
                       epilogue=_sb_epilogue,
                       extra=(q_norm.reshape(1, HEAD_DIM), k_norm.reshape(1, HEAD_DIM)))
    o = _sb_attention(qkv, batch=batch, seq=seq, tile=256, nsub=4)
    return _matmul_residual(o, w_out.astype(BF16), x, tm=1024, tn=1024)


def _rope_tables(seq):
    half = MLA_ROPE // 2
    inv_freq = ROPE_THETA ** (-jnp.arange(0, MLA_ROPE, 2, dtype=F32) / MLA_ROPE)
    ang = jnp.arange(seq, dtype=F32)[:, None] * inv_freq[None, :]
    cos, sin, zero = jnp.cos(ang), jnp.sin(ang), jnp.zeros((seq, half), F32)
    pad = jnp.zeros((seq, LANES - MLA_ROPE), F32)
    return (jnp.concatenate([cos, cos, pad], axis=1),
            jnp.concatenate([-sin, zero, pad], axis=1),
            jnp.concatenate([zero, sin, pad], axis=1))


def _mla_mixer(x, mix_norm, w_down, q_a_norm, kv_a_norm, w_uq, w_ukv, q_nope_norm, q_rope_norm,
               k_nope_norm, k_rope_norm, w_out, *, batch, seq):
    down_w = jnp.pad(w_down, ((0, 0), (0, 512 - w_down.shape[1]))).astype(BF16)
    wq = jnp.pad(w_uq.reshape(MLA_Q_RANK, N_HEADS, MLA_QK),
                 ((0, 0), (0, 0), (0, MLA_QPAD - MLA_QK))).reshape(MLA_Q_RANK, N_HEADS * MLA_QPAD).astype(BF16)
    wkv = w_ukv.reshape(MLA_KV_RANK, N_HEADS, 2, HEAD_DIM).transpose(0, 2, 1, 3).reshape(
        MLA_KV_RANK, 2 * N_HEADS * HEAD_DIM).astype(BF16)
    down = _norm_matmul(x, mix_norm, down_w, tm=1024, tn=512, out_dtype=F32)
    gains = (q_a_norm.reshape(1, -1), kv_a_norm.reshape(1, -1), q_nope_norm.reshape(1, -1),
             _pad_lanes(q_rope_norm), k_nope_norm.reshape(1, -1), _pad_lanes(k_rope_norm))
    q, kn, v, kr = _mla_prep(down, wq, wkv, gains, _rope_tables(seq), seq=seq, tm=512)
    o = _mla_attention(q, kn, kr, v, batch=batch, seq=seq, tile=256, nsub=4)
    return _matmul_residual(o, w_out.astype(BF16), x, tm=1024, tn=1024)


def kernel(x, l0_mix_norm, l0_dn_w_in, l0_dn_conv_w, l0_dn_a_log, l0_dn_dt_bias, l0_dn_out_norm, l0_dn_w_out, l0_ffn_norm, l0_ffn_w_gate_up, l0_ffn_w_down, l1_mix_norm, l1_sb_w_qkv, l1_sb_q_norm, l1_sb_k_norm, l1_sb_w_out, l1_ffn_norm, l1_ffn_w_gate_up, l1_ffn_w_down, l2_mix_norm, l2_mla_w_down, l2_mla_q_a_norm, l2_mla_kv_a_norm, l2_mla_w_uq, l2_mla_w_ukv, l2_mla_q_nope_norm, l2_mla_q_rope_norm, l2_mla_k_nope_norm, l2_mla_k_rope_norm, l2_mla_w_out, l2_ffn_norm, l2_ffn_w_gate_up, l2_ffn_w_down, l3_mix_norm, l3_dn_w_in, l3_dn_conv_w, l3_dn_a_log, l3_dn_dt_bias, l3_dn_out_norm, l3_dn_w_out, l3_ffn_norm, l3_ffn_w_gate_up, l3_ffn_w_down):
    batch, seq, d_model = x.shape
    shape = dict(batch=batch, seq=seq)
    ffn = lambda y, g, wgu, wd: _ffn(y, g, wgu.astype(BF16), wd.astype(BF16), tm=512, th=1408)
    y = x.reshape(batch * seq, d_model)
    y = _deltanet_mixer(y, l0_mix_norm, l0_dn_w_in, l0_dn_conv_w, l0_dn_a_log, l0_dn_dt_bias, l0_dn_out_norm,
                        l0_dn_w_out, **shape)
    y = ffn(y, l0_ffn_norm, l0_ffn_w_gate_up, l0_ffn_w_down)
    y = _stick_breaking_mixer(y, l1_mix_norm, l1_sb_w_qkv, l1_sb_q_norm, l1_sb_k_norm, l1_sb_w_out, **shape)
    y = ffn(y, l1_ffn_norm, l1_ffn_w_gate_up, l1_ffn_w_down)
    y = _mla_mixer(y, l2_mix_norm, l2_mla_w_down, l2_mla_q_a_norm, l2_mla_kv_a_norm, l2_mla_w_uq, l2_mla_w_ukv,
                   l2_mla_q_nope_norm, l2_mla_q_rope_norm, l2_mla_k_nope_norm, l2_mla_k_rope_norm, l2_mla_w_out,
                   **shape)
    y = ffn(y, l2_ffn_norm, l2_ffn_w_gate_up, l2_ffn_w_down)
    y = _deltanet_mixer(y, l3_mix_norm, l3_dn_w_in, l3_dn_conv_w, l3_dn_a_log, l3_dn_dt_bias, l3_dn_out_norm,
                        l3_dn_w_out, **shape)
    y = ffn(y, l3_ffn_norm, l3_ffn_w_gate_up, l3_ffn_w_down)
    return y.reshape(batch, seq, d_model)
```

```python
import functools

import jax
import jax.numpy as jnp
from jax import lax
from jax.experimental import pallas as pl
from jax.experimental.pallas import tpu as pltpu

F32 = jnp.float32
BF16 = jnp.bfloat16

NORM_EPS = 1e-6
LANES = 128
HEAD_DIM = 128
N_HEADS = 8
DN_CONV = 4
DN_CHUNK = 128
MLA_ROPE = 64
MLA_QK = 192
MLA_Q_RANK = 256
MLA_KV_RANK = 128
MLA_QPAD = 256
ROPE_THETA = 10000.0
LOG2E = 1.4426950408889634
MLA_MAX_BOUND = 48.0
VMEM_LIMIT = 56 * 1024 * 1024
FFN_ROWS = 512


def _params(*sem):
    return pltpu.CompilerParams(dimension_semantics=sem, vmem_limit_bytes=VMEM_LIMIT)


def _rms(x):
    return x * lax.rsqrt(jnp.mean(x * x, axis=-1, keepdims=True) + NORM_EPS)


def _sigmoid(x):
    return 1.0 / (1.0 + jnp.exp(-x))


def _silu(x):
    return x / (1.0 + jnp.exp2(x * (-LOG2E)))


def _softplus(x):
    return jnp.maximum(x, 0.0) + jnp.log(1.0 + jnp.exp(-jnp.abs(x)))


def _softplus2(x):
    return jnp.maximum(x, 0.0) + jnp.log(1.0 + jnp.exp2(-jnp.abs(x))) * LOG2E


def _dot(a, b):
    return jnp.dot(a, b, preferred_element_type=F32)


def _dot_nt(a, b):
    return lax.dot_general(a, b, (((1,), (1,)), ((), ())), preferred_element_type=F32)


def _split3(x):
    hi = x.astype(BF16)
    r = x - hi.astype(F32)
    mid = r.astype(BF16)
    lo = (r - mid.astype(F32)).astype(BF16)
    return hi, mid, lo


def _plain_epilogue(acc, j, extra, o_ref):
    del j, extra
    o_ref[...] = acc.astype(o_ref.dtype)


def _sb_epilogue(acc, j, extra, o_ref):
    qg_ref, kg_ref = extra

    def head_norm(gain, scale):
        for h in range(acc.shape[1] // HEAD_DIM):
            sl = slice(h * HEAD_DIM, (h + 1) * HEAD_DIM)
            o_ref[:, sl] = (_rms(acc[:, sl]) * gain * scale).astype(o_ref.dtype)

    @pl.when(j == 0)
    def _():
        head_norm(qg_ref[...], HEAD_DIM ** -0.5 * LOG2E)

    @pl.when(j == 1)
    def _():
        head_norm(kg_ref[...], 1.0)

    @pl.when(j == 2)
    def _():
        o_ref[...] = acc.astype(o_ref.dtype)


def _norm_matmul_body(x_ref, g_ref, w_ref, *rest, epilogue, n_extra):
    extra, o_ref, h_ref = rest[:n_extra], rest[n_extra], rest[n_extra + 1]
    j = pl.program_id(1)

    @pl.when(j == 0)
    def _():
        h_ref[...] = (_rms(x_ref[...]) * g_ref[...]).astype(BF16)

    epilogue(_dot(h_ref[...], w_ref[...]), j, extra, o_ref)


def _norm_matmul(x, gain, w, *, tm, tn, out_dtype, epilogue=_plain_epilogue, extra=()):
    m, d = x.shape
    n = w.shape[1]
    tm = min(tm, m)
    extra_specs = [pl.BlockSpec(e.shape, lambda i, j: (0, 0)) for e in extra]
    return pl.pallas_call(
        functools.partial(_norm_matmul_body, epilogue=epilogue, n_extra=len(extra)),
        grid=(m // tm, n // tn),
        in_specs=[pl.BlockSpec((tm, d), lambda i, j: (i, 0)),
                  pl.BlockSpec((1, d), lambda i, j: (0, 0)),
                  pl.BlockSpec((d, tn), lambda i, j: (0, j))] + extra_specs,
        out_specs=pl.BlockSpec((tm, tn), lambda i, j: (i, j)),
        out_shape=jax.ShapeDtypeStruct((m, n), out_dtype),
        scratch_shapes=[pltpu.VMEM((tm, d), BF16)],
        compiler_params=_params("parallel", "arbitrary"),
        name="norm_matmul",
    )(x, gain.reshape(1, d), w, *extra)


def _proj_ffn_body(a_ref, wo_ref, x_ref, g_ref, wg_ref, wu_ref, wd_ref, o_ref, h_ref):
    @pl.when(pl.program_id(1) == 0)
    def _():
        y = x_ref[...] + _dot(a_ref[...], wo_ref[...])
        h_ref[...] = (_rms(y) * g_ref[...]).astype(BF16)
        o_ref[...] = y

    h = h_ref[...]
    gate = _dot(h, wg_ref[...])
    up = _dot(h, wu_ref[...])
    act = (_silu(gate) * up).astype(BF16)
    o_ref[...] += _dot(act, wd_ref[...])


def _proj_ffn(a, w_out, x, gain, w_gate_up, w_down, *, tm, th):
    m, d = x.shape
    ka = a.shape[1]
    hidden = w_down.shape[0]
    nh = hidden // th
    tm = min(tm, m)
    once = dict(pipeline_mode=pl.Buffered(1))
    resident = once if nh == 1 else {}
    return pl.pallas_call(
        _proj_ffn_body,
        grid=(m // tm, nh),
        in_specs=[pl.BlockSpec((tm, ka), lambda i, j: (i, 0)),
                  pl.BlockSpec((ka, d), lambda i, j: (0, 0), **once),
                  pl.BlockSpec((tm, d), lambda i, j: (i, 0)),
                  pl.BlockSpec((1, d), lambda i, j: (0, 0)),
                  pl.BlockSpec((d, th), lambda i, j: (0, j), **resident),
                  pl.BlockSpec((d, th), lambda i, j: (0, nh + j), **resident),
                  pl.BlockSpec((th, d), lambda i, j: (j, 0), **resident)],
        out_specs=pl.BlockSpec((tm, d), lambda i, j: (i, 0)),
        out_shape=jax.ShapeDtypeStruct((m, d), F32),
        scratch_shapes=[pltpu.VMEM((tm, d), BF16)],
        compiler_params=_params("parallel", "arbitrary"),
        name="proj_ffn",
    )(a, w_out, x, gain.reshape(1, d), w_gate_up, w_gate_up, w_down)


def _sb_attn_body(q_ref, k_ref, v_ref, o_ref, *, tile, nsub, kunroll):
    qi = pl.program_id(2)
    qs = [q_ref[s * tile:(s + 1) * tile, :] for s in range(nsub)]
    row = lax.broadcasted_iota(jnp.int32, (tile, tile), 0)
    col = lax.broadcasted_iota(jnp.int32, (tile, tile), 1)
    suffix_ones = (row >= col).astype(BF16)
    suffix_ones2 = jnp.concatenate([suffix_ones, suffix_ones], axis=0)
    past = col < row

    def sweep(starts, subs, stays, accs, diag_sub):
        ks = [k_ref[pl.ds(st, tile), :] for st in starts]
        vs = [v_ref[pl.ds(st, tile), :] for st in starts]
        jobs = [(t, s) for t in range(len(starts)) for s in subs]
        zs = [_dot_nt(qs[s], ks[t]) for t, s in jobs]
        sps = [_softplus2(z) for z in zs]
        sps = [jnp.where(past, sp, 0.0) if s == diag_sub else sp for (_, s), sp in zip(jobs, sps)]
        his = [sp.astype(BF16) for sp in sps]
        los = [(sp - hi.astype(F32)).astype(BF16) for sp, hi in zip(sps, his)]
        cums = [_dot(jnp.concatenate([hi, lo], axis=1), suffix_ones2) for hi, lo in zip(his, los)]
        stays, accs = list(stays), list(accs)
        ps = []
        for (_, s), z, cum in zip(jobs, zs, cums):
            p = jnp.exp2(z - cum - stays[s])
            ps.append(jnp.where(past, p, 0.0) if s == diag_sub else p)
            stays[s] = stays[s] + cum[:, 0:1]
        for (t, s), p in zip(jobs, ps):
            accs[s] = accs[s] + _dot(p.astype(BF16), vs[t])
        return stays, accs

    stays = [jnp.zeros((tile, 1), F32)] * nsub
    accs = [jnp.zeros((tile, HEAD_DIM), F32)] * nsub
    base = qi * nsub
    for j in reversed(range(nsub)):
        start = pl.multiple_of((base + j) * tile, tile)
        stays, accs = sweep([start], list(range(j, nsub)), stays, accs, j)

    def body(i, carry):
        first = base - 1 - i * kunroll
        starts = [pl.multiple_of((first - u) * tile, tile) for u in range(kunroll)]
        st, ac = sweep(starts, list(range(nsub)), carry[:nsub], carry[nsub:], None)
        return tuple(st) + tuple(ac)

    carry = lax.fori_loop(0, base // kunroll, body, tuple(stays) + tuple(accs))
    for s in range(nsub):
        o_ref[s * tile:(s + 1) * tile, :] = carry[nsub + s].astype(o_ref.dtype)


def _sb_attention(qkv, *, batch, seq, tile, nsub, kunroll):
    m = qkv.shape[0]
    nq = seq // (tile * nsub)
    assert nsub % kunroll == 0
    return pl.pallas_call(
        functools.partial(_sb_attn_body, tile=tile, nsub=nsub, kunroll=kunroll),
        grid=(batch, N_HEADS, nq),
        in_specs=[pl.BlockSpec((tile * nsub, HEAD_DIM), lambda b, h, i: (b * nq + i, h)),
                  pl.BlockSpec((seq, HEAD_DIM), lambda b, h, i: (b, N_HEADS + h)),
                  pl.BlockSpec((seq, HEAD_DIM), lambda b, h, i: (b, 2 * N_HEADS + h))],
        out_specs=pl.BlockSpec((tile * nsub, HEAD_DIM), lambda b, h, i: (b * nq + i, h)),
        out_shape=jax.ShapeDtypeStruct((m, N_HEADS * HEAD_DIM), BF16),
        compiler_params=_params("parallel", "parallel", "arbitrary"),
        name="sb_attention",
    )(qkv, qkv, qkv)


def _rope(x, cos_t, sin_a, sin_b):
    return x * cos_t + pltpu.roll(x, 96, 1) * sin_a + pltpu.roll(x, 32, 1) * sin_b


def _rms_rope_part(x, gain):
    ms = jnp.sum(x * x, axis=-1, keepdims=True) * (1.0 / MLA_ROPE)
    return x * lax.rsqrt(ms + NORM_EPS) * gain


def _mla_score_scale():
    return MLA_QK ** -0.5 * LOG2E


def _mla_score_bound(q_nope_g, q_rope_g, k_nope_g, k_rope_g):
    gmax = lambda g: jnp.max(jnp.abs(g), axis=-1, keepdims=True)
    return 1.02 * _mla_score_scale() * (HEAD_DIM * gmax(q_nope_g) * gmax(k_nope_g)
                                         + MLA_ROPE * gmax(q_rope_g) * gmax(k_rope_g))


def _mla_prep_body(d_ref, wq_ref, wkv_ref, qa_ref, kva_ref, qn_ref, qr_ref, kn_ref, kr_ref,
                   cos_ref, sa_ref, sb_ref, q_out, kn_out, v_out, kr_out):
    d = d_ref[...]
    cos_t, sin_a, sin_b = cos_ref[...], sa_ref[...], sb_ref[...]
    scale = _mla_score_scale()
    lane = lax.broadcasted_iota(jnp.int32, (1, LANES), 1)
    shift = _mla_score_bound(qn_ref[...], qr_ref[...], kn_ref[...], kr_ref[...])
    q_bias = jnp.where(lane == MLA_ROPE, -shift, 0.0)
    k_bias = jnp.where(lane == MLA_ROPE, 1.0, 0.0)
    cq = (_rms(d[:, :MLA_Q_RANK]) * qa_ref[...]).astype(BF16)
    ckv = (_rms(d[:, MLA_Q_RANK:MLA_Q_RANK + MLA_KV_RANK]) * kva_ref[...]).astype(BF16)
    qp = _dot(cq, wq_ref[...])
    kv = _dot(ckv, wkv_ref[...])
    for h in range(N_HEADS):
        c0 = h * MLA_QPAD
        nope = _rms(qp[:, c0:c0 + HEAD_DIM]) * qn_ref[...]
        q_out[:, c0:c0 + HEAD_DIM] = (nope * scale).astype(BF16)
        rot = _rope(_rms_rope_part(qp[:, c0 + HEAD_DIM:c0 + MLA_QPAD], qr_ref[...]), cos_t, sin_a, sin_b)
        q_out[:, c0 + HEAD_DIM:c0 + MLA_QPAD] = (rot * scale + q_bias).astype(BF16)
        ks = slice(h * HEAD_DIM, (h + 1) * HEAD_DIM)
        kn_out[:, ks] = (_rms(kv[:, ks]) * kn_ref[...]).astype(BF16)
    v_out[...] = kv[:, N_HEADS * HEAD_DIM:].astype(BF16)
    k_rope = d[:, MLA_Q_RANK + MLA_KV_RANK:]
    kr_out[...] = (_rope(_rms_rope_part(k_rope, kr_ref[...]), cos_t, sin_a, sin_b) + k_bias).astype(BF16)


def _mla_prep(down, wq, wkv, gains, tables, *, seq, tm):
    m = down.shape[0]
    nt = seq // tm
    full = lambda a: pl.BlockSpec(a.shape, lambda i: (0, 0))
    tab = pl.BlockSpec((tm, LANES), lambda i: (i % nt, 0))
    row = lambda n: pl.BlockSpec((tm, n), lambda i: (i, 0))
    return pl.pallas_call(
        _mla_prep_body,
        grid=(m // tm,),
        in_specs=[row(down.shape[1]), full(wq), full(wkv)] + [full(g) for g in gains] + [tab, tab, tab],
        out_specs=[row(N_HEADS * MLA_QPAD), row(N_HEADS * HEAD_DIM), row(N_HEADS * HEAD_DIM), row(LANES)],
        out_shape=[jax.ShapeDtypeStruct((m, N_HEADS * MLA_QPAD), BF16),
                   jax.ShapeDtypeStruct((m, N_HEADS * HEAD_DIM), BF16),
                   jax.ShapeDtypeStruct((m, N_HEADS * HEAD_DIM), BF16),
                   jax.ShapeDtypeStruct((m, LANES), BF16)],
        compiler_params=_params("parallel"),
        name="mla_prep",
    )(down, wq, wkv, *gains, *tables)


def _mla_attn_body(q_ref, kn_ref, kr_ref, v_ref, o_ref, *, tile, nsub):
    qi = pl.program_id(2)
    qs = [q_ref[s * tile:(s + 1) * tile, :] for s in range(nsub)]
    row = lax.broadcasted_iota(jnp.int32, (tile, tile), 0)
    col = lax.broadcasted_iota(jnp.int32, (tile, tile), 1)
    visible = col <= row

    def sweep(start, subs, carry, diag_sub):
        k = jnp.concatenate([kn_ref[pl.ds(start, tile), :], kr_ref[pl.ds(start, tile), :]], axis=1)
        v = v_ref[pl.ds(start, tile), :]
        ss = [_dot_nt(qs[s], k) for s in subs]
        ss = [jnp.where(visible, x, -1e30) if s == diag_sub else x for s, x in zip(subs, ss)]
        m_new = [jnp.maximum(carry[s][0], jnp.max(x, axis=-1, keepdims=True)) for s, x in zip(subs, ss)]
        ps = [jnp.exp2(x - mn) for x, mn in zip(ss, m_new)]
        carry = list(carry)
        for s, p, mn in zip(subs, ps, m_new):
            m_run, l_run, acc = carry[s]
            alpha = jnp.exp2(m_run - mn)
            carry[s] = (mn, alpha * l_run + jnp.sum(p, axis=-1, keepdims=True),
                        alpha * acc + _dot(p.astype(BF16), v))
        return carry

    def flat(carry):
        return tuple(x for c in carry for x in c)

    def body(i, fc):
        carry = [tuple(fc[3 * s:3 * s + 3]) for s in range(nsub)]
        return flat(sweep(pl.multiple_of(i * tile, tile), list(range(nsub)), carry, None))

    base = qi * nsub
    init = [(jnp.full((tile, 1), -1e30, F32), jnp.zeros((tile, 1), F32),
             jnp.zeros((tile, HEAD_DIM), F32))] * nsub
    fc = lax.fori_loop(0, base, body, flat(init))
    carry = [tuple(fc[3 * s:3 * s + 3]) for s in range(nsub)]
    for j in range(nsub):
        start = pl.multiple_of((base + j) * tile, tile)
        carry = sweep(start, list(range(j, nsub)), carry, j)
    for s in range(nsub):
        o_ref[s * tile:(s + 1) * tile, :] = (carry[s][2] / carry[s][1]).astype(o_ref.dtype)


def _mla_attn_bounded_body(q_ref, kn_ref, kr_ref, v_ref, o_ref, *, tile, nsub, kunroll):
    qi = pl.program_id(2)
    qs = [q_ref[s * tile:(s + 1) * tile, :] for s in range(nsub)]
    row = lax.broadcasted_iota(jnp.int32, (tile, tile), 0)
    col = lax.broadcasted_iota(jnp.int32, (tile, tile), 1)
    visible = col <= row
    ones_col = (lax.broadcasted_iota(jnp.int32, (tile, LANES), 1) == 0).astype(BF16)

    def sweep(starts, subs, accs, diag_sub):
        ks = [jnp.concatenate([kn_ref[pl.ds(st, tile), :], kr_ref[pl.ds(st, tile), :]], axis=1) for st in starts]
        vs = [jnp.concatenate([v_ref[pl.ds(st, tile), :], ones_col], axis=1) for st in starts]
        jobs = [(t, s) for t in range(len(starts)) for s in subs]
        ss = [_dot_nt(qs[s], ks[t]) for t, s in jobs]
        ss = [jnp.where(visible, x, -1e30) if s == diag_sub else x for (_, s), x in zip(jobs, ss)]
        ps = [jnp.exp2(x).astype(BF16) for x in ss]
        accs = list(accs)
        for (t, s), p in zip(jobs, ps):
            accs[s] = accs[s] + _dot(p, vs[t])
        return accs

    def body(i, accs):
        starts = [pl.multiple_of((i * kunroll + u) * tile, tile) for u in range(kunroll)]
        return tuple(sweep(starts, list(range(nsub)), accs, None))

    base = qi * nsub
    accs = lax.fori_loop(0, base // kunroll, body, (jnp.zeros((tile, 2 * HEAD_DIM), F32),) * nsub)
    for j in range(nsub):
        start = pl.multiple_of((base + j) * tile, tile)
        accs = sweep([start], list(range(j, nsub)), accs, j)
    for s in range(nsub):
        o_ref[s * tile:(s + 1) * tile, :] = (
            accs[s][:, :HEAD_DIM] / accs[s][:, HEAD_DIM:HEAD_DIM + 1]).astype(o_ref.dtype)


def _mla_attention(q, kn, kr, v, *, bounded, batch, seq, tile, nsub, kunroll):
    m = q.shape[0]
    nq = seq // (tile * nsub)
    assert nsub % kunroll == 0
    body = (functools.partial(_mla_attn_bounded_body, kunroll=kunroll) if bounded else _mla_attn_body)
    return pl.pallas_call(
        functools.partial(body, tile=tile, nsub=nsub),
        grid=(batch, N_HEADS, nq),
        in_specs=[pl.BlockSpec((tile * nsub, MLA_QPAD), lambda b, h, i: (b * nq + i, h)),
                  pl.BlockSpec((seq, HEAD_DIM), lambda b, h, i: (b, h)),
                  pl.BlockSpec((seq, LANES), lambda b, h, i: (b, 0)),
                  pl.BlockSpec((seq, HEAD_DIM), lambda b, h, i: (b, h))],
        out_specs=pl.BlockSpec((tile * nsub, HEAD_DIM), lambda b, h, i: (b * nq + i, h)),
        out_shape=jax.ShapeDtypeStruct((m, N_HEADS * HEAD_DIM), BF16),
        compiler_params=_params("parallel", "parallel", "arbitrary"),
        name="mla_attention_bounded" if bounded else "mla_attention_online",
    )(q, kn, kr, v)


def _dn_body(x_ref, ab_ref, cw_ref, alog_ref, dtb_ref, onorm_ref, o_ref, state_ref, ext_ref):
    c = DN_CHUNK
    width = N_HEADS * HEAD_DIM

    @pl.when(pl.program_id(1) == 0)
    def _():
        state_ref[...] = jnp.zeros_like(state_ref)
        ext_ref[0:8, :] = jnp.zeros((8, 3 * width), F32)

    ext_ref[8:8 + c, :] = x_ref[:, 0:3 * width].astype(F32)

    def conv_silu(col):
        sl = slice(col, col + HEAD_DIM)
        y = cw_ref[DN_CONV - 1:DN_CONV, sl] * ext_ref[8:8 + c, sl]
        for j in range(DN_CONV - 1):
            y = y + cw_ref[j:j + 1, sl] * ext_ref[5 + j:5 + j + c, sl]
        return _silu(y)

    ab = ab_ref[...]
    g_all = -jnp.exp(alog_ref[...]) * _softplus(ab + dtb_ref[...])
    beta_all = _sigmoid(ab)
    row = lax.broadcasted_iota(jnp.int32, (c, c), 0)
    col = lax.broadcasted_iota(jnp.int32, (c, c), 1)
    causal = row >= col
    strict = row > col
    eye = (row == col).astype(F32)
    g_parts = _split3(g_all)
    prefix_ones = causal.astype(BF16)
    all_ones = jnp.ones((c, c), BF16)
    gc_all = sum(_dot(prefix_ones, p) for p in g_parts)
    gt_all = sum(_dot(all_ones, p) for p in g_parts)
    gc_rows = gc_all.T
    levels = []
    b = 1
    while b < c:
        shift = b.bit_length() - 1
        levels.append(((row >> (shift + 1)) == (col >> (shift + 1)))
                      & (((row >> shift) & 1) == 1) & (((col >> shift) & 1) == 0))
        b *= 2

    heads = range(N_HEADS)
    q, k, v = [], [], []
    for h in heads:
        qh = conv_silu(h * HEAD_DIM)
        kh = conv_silu(width + h * HEAD_DIM)
        q.append(qh * lax.rsqrt(jnp.sum(qh * qh, axis=-1, keepdims=True) + NORM_EPS) * (HEAD_DIM ** -0.5))
        k.append(kh * lax.rsqrt(jnp.sum(kh * kh, axis=-1, keepdims=True) + NORM_EPS))
        v.append(conv_silu(2 * width + h * HEAD_DIM))
    g_col = [gc_all[:, h:h + 1] for h in heads]
    g_tot = [gt_all[:, h:h + 1] for h in heads]
    beta = [beta_all[:, N_HEADS + h:N_HEADS + h + 1] for h in heads]
    decay = [jnp.exp(jnp.where(causal, g_col[h] - gc_rows[h:h + 1, :], 0.0)) for h in heads]
    k_bf = [k[h].astype(BF16) for h in heads]
    k_beta = [k[h] * beta[h] for h in heads]
    kk = [_dot_nt(k_beta[h].astype(BF16), k_bf[h]) for h in heads]
    qk = [_dot_nt(q[h].astype(BF16), k_bf[h]) for h in heads]
    lower = [jnp.where(strict, kk[h] * decay[h], 0.0) for h in heads]
    inv = [eye - jnp.where(levels[0], lower[h], 0.0) for h in heads]
    lower_bf = [lower[h].astype(BF16) for h in heads]
    for mask in levels[1:]:
        mask_bf = mask.astype(BF16)
        inv_bf = [inv[h].astype(BF16) for h in heads]
        step = [_dot(inv_bf[h], lower_bf[h] * mask_bf) for h in heads]
        inv = [inv[h] - _dot(step[h].astype(BF16), inv_bf[h]) for h in heads]
    exp_g = [jnp.exp(g_col[h]) for h in heads]
    uw = [_dot(inv[h].astype(BF16),
               jnp.concatenate([v[h] * beta[h], k_beta[h] * exp_g[h]], axis=1).astype(BF16)) for h in heads]
    state = [state_ref[h] for h in heads]
    ws_qs = [_dot(jnp.concatenate([uw[h][:, HEAD_DIM:], q[h] * exp_g[h]], axis=0).astype(BF16),
                  state[h].astype(BF16)) for h in heads]
    v_new = [(uw[h][:, :HEAD_DIM] - ws_qs[h][:c]).astype(BF16) for h in heads]
    attn = [jnp.where(causal, qk[h] * decay[h], 0.0).astype(BF16) for h in heads]
    out = [ws_qs[h][c:] + _dot(attn[h], v_new[h]) for h in heads]
    for h in heads:
        k_dec = k[h] * jnp.exp(g_tot[h] - g_col[h])
        state_ref[h] = state[h] * jnp.exp(g_tot[h]) + _dot(k_dec.T.astype(BF16), v_new[h])
    for h in heads:
        z = x_ref[:, 3 * width + h * HEAD_DIM:3 * width + (h + 1) * HEAD_DIM].astype(F32)
        o_ref[:, h * HEAD_DIM:(h + 1) * HEAD_DIM] = (
            _rms(out[h]) * onorm_ref[...] * _silu(z)).astype(o_ref.dtype)

    ext_ref[0:8, :] = ext_ref[c:c + 8, :]


def _dn_core(qkvz, ab, conv_w, a_log, dt_bias, out_norm, *, batch, seq):
    m = qkvz.shape[0]
    nc = seq // DN_CHUNK
    width = N_HEADS * HEAD_DIM
    full = lambda a: pl.BlockSpec(a.shape, lambda b, t: (0, 0))
    return pl.pallas_call(
        _dn_body,
        grid=(batch, nc),
        in_specs=[pl.BlockSpec((DN_CHUNK, 4 * width), lambda b, t: (b * nc + t, 0)),
                  pl.BlockSpec((DN_CHUNK, LANES), lambda b, t: (b * nc + t, 0)),
                  full(conv_w), full(a_log), full(dt_bias), full(out_norm)],
        out_specs=pl.BlockSpec((DN_CHUNK, width), lambda b, t: (b * nc + t, 0)),
        out_shape=jax.ShapeDtypeStruct((m, width), BF16),
        scratch_shapes=[pltpu.VMEM((N_HEADS, HEAD_DIM, HEAD_DIM), F32),
                        pltpu.VMEM((DN_CHUNK + 8, 3 * width), F32)],
        compiler_params=_params("parallel", "arbitrary"),
        name="dn_core",
    )(qkvz, ab, conv_w, a_log, dt_bias, out_norm)


def _pad_lanes(vec, n=LANES):
    return jnp.pad(vec.astype(F32), (0, n - vec.shape[0])).reshape(1, n)


def _deltanet_mixer(x, mix_norm, w_in, conv_w, a_log, dt_bias, out_norm, *, batch, seq):
    width = N_HEADS * HEAD_DIM
    w_main = w_in[:, :4 * width].astype(BF16)
    w_gate = jnp.pad(w_in[:, 4 * width:], ((0, 0), (0, LANES - 2 * N_HEADS))).astype(BF16)
    qkvz = _norm_matmul(x, mix_norm, w_main, tm=1024, tn=1024, out_dtype=BF16)
    ab = _norm_matmul(x, mix_norm, w_gate, tm=1024, tn=LANES, out_dtype=F32)
    o = _dn_core(qkvz, ab, conv_w, _pad_lanes(a_log), _pad_lanes(dt_bias), out_norm.reshape(1, HEAD_DIM),
                 batch=batch, seq=seq)
    return o


def _stick_breaking_mixer(x, mix_norm, w_qkv, q_norm, k_norm, *, batch, seq):
    qkv = _norm_matmul(x, mix_norm, w_qkv.astype(BF16), tm=1024, tn=N_HEADS * HEAD_DIM, out_dtype=BF16,
                       epilogue=_sb_epilogue,
                       extra=(q_norm.reshape(1, HEAD_DIM), k_norm.reshape(1, HEAD_DIM)))
    o = _sb_attention(qkv, batch=batch, seq=seq, tile=256, nsub=4, kunroll=2)
    return o


def _rope_tables(seq):
    half = MLA_ROPE // 2
    inv_freq = ROPE_THETA ** (-jnp.arange(0, MLA_ROPE, 2, dtype=F32) / MLA_ROPE)
    ang = jnp.arange(seq, dtype=F32)[:, None] * inv_freq[None, :]
    cos, sin, zero = jnp.cos(ang), jnp.sin(ang), jnp.zeros((seq, half), F32)
    pad = jnp.zeros((seq, LANES - MLA_ROPE), F32)
    return (jnp.concatenate([cos, cos, pad], axis=1),
            jnp.concatenate([-sin, zero, pad], axis=1),
            jnp.concatenate([zero, sin, pad], axis=1))


def _mla_mixer(x, mix_norm, w_down, q_a_norm, kv_a_norm, w_uq, w_ukv, q_nope_norm, q_rope_norm,
               k_nope_norm, k_rope_norm, *, batch, seq):
    down_w = jnp.pad(w_down, ((0, 0), (0, 512 - w_down.shape[1]))).astype(BF16)
    wq = jnp.pad(w_uq.reshape(MLA_Q_RANK, N_HEADS, MLA_QK),
                 ((0, 0), (0, 0), (0, MLA_QPAD - MLA_QK))).reshape(MLA_Q_RANK, N_HEADS * MLA_QPAD).astype(BF16)
    wkv = w_ukv.reshape(MLA_KV_RANK, N_HEADS, 2, HEAD_DIM).transpose(0, 2, 1, 3).reshape(
        MLA_KV_RANK, 2 * N_HEADS * HEAD_DIM).astype(BF16)
    down = _norm_matmul(x, mix_norm, down_w, tm=1024, tn=512, out_dtype=F32)
    gains = (q_a_norm.reshape(1, -1), kv_a_norm.reshape(1, -1), q_nope_norm.reshape(1, -1),
             _pad_lanes(q_rope_norm), k_nope_norm.reshape(1, -1), _pad_lanes(k_rope_norm))
    q, kn, v, kr = _mla_prep(down, wq, wkv, gains, _rope_tables(seq), seq=seq, tm=512)
    attend = functools.partial(_mla_attention, q, kn, kr, v, batch=batch, seq=seq, tile=256, nsub=4,
                               kunroll=2)
    bound = _mla_score_bound(gains[2], gains[3], gains[4], gains[5])[0, 0]
    o = lax.cond(bound < MLA_MAX_BOUND, lambda: attend(bounded=True), lambda: attend(bounded=False))
    return o


def kernel(x, l0_mix_norm, l0_dn_w_in, l0_dn_conv_w, l0_dn_a_log, l0_dn_dt_bias, l0_dn_out_norm, l0_dn_w_out, l0_ffn_norm, l0_ffn_w_gate_up, l0_ffn_w_down, l1_mix_norm, l1_sb_w_qkv, l1_sb_q_norm, l1_sb_k_norm, l1_sb_w_out, l1_ffn_norm, l1_ffn_w_gate_up, l1_ffn_w_down, l2_mix_norm, l2_mla_w_down, l2_mla_q_a_norm, l2_mla_kv_a_norm, l2_mla_w_uq, l2_mla_w_ukv, l2_mla_q_nope_norm, l2_mla_q_rope_norm, l2_mla_k_nope_norm, l2_mla_k_rope_norm, l2_mla_w_out, l2_ffn_norm, l2_ffn_w_gate_up, l2_ffn_w_down, l3_mix_norm, l3_dn_w_in, l3_dn_conv_w, l3_dn_a_log, l3_dn_dt_bias, l3_dn_out_norm, l3_dn_w_out, l3_ffn_norm, l3_ffn_w_gate_up, l3_ffn_w_down):
    batch, seq, d_model = x.shape
    shape = dict(batch=batch, seq=seq)
    def close_layer(mixed, w_out, y, g, wgu, wd):
        return _proj_ffn(mixed, w_out.astype(BF16), y, g, wgu.astype(BF16), wd.astype(BF16),
                         tm=FFN_ROWS, th=wd.shape[0])

    y = x.reshape(batch * seq, d_model)
    o = _deltanet_mixer(y, l0_mix_norm, l0_dn_w_in, l0_dn_conv_w, l0_dn_a_log, l0_dn_dt_bias, l0_dn_out_norm,
                        **shape)
    y = close_layer(o, l0_dn_w_out, y, l0_ffn_norm, l0_ffn_w_gate_up, l0_ffn_w_down)
    o = _stick_breaking_mixer(y, l1_mix_norm, l1_sb_w_qkv, l1_sb_q_norm, l1_sb_k_norm, **shape)
    y = close_layer(o, l1_sb_w_out, y, l1_ffn_norm, l1_ffn_w_gate_up, l1_ffn_w_down)
    o = _mla_mixer(y, l2_mix_norm, l2_mla_w_down, l2_mla_q_a_norm, l2_mla_kv_a_norm, l2_mla_w_uq, l2_mla_w_ukv,
                   l2_mla_q_nope_norm, l2_mla_q_rope_norm, l2_mla_k_nope_norm, l2_mla_k_rope_norm, **shape)
    y = close_layer(o, l2_mla_w_out, y, l2_ffn_norm, l2_ffn_w_gate_up, l2_ffn_w_down)
    o = _deltanet_mixer(y, l3_mix_norm, l3_dn_w_in, l3_dn_conv_w, l3_dn_a_log, l3_dn_dt_bias, l3_dn_out_norm,
                        **shape)
    y = close_layer(o, l3_dn_w_out, y, l3_ffn_norm, l3_ffn_w_gate_up, l3_ffn_w_down)
    return y.reshape(batch, seq, d_model)
```

```python
import functools

import jax
import jax.numpy as jnp
from jax import lax
from jax.experimental import pallas as pl
from jax.experimental.pallas import tpu as pltpu

F32 = jnp.float32
BF16 = jnp.bfloat16

NORM_EPS = 1e-6
LANES = 128
HEAD_DIM = 128
N_HEADS = 8
DN_CONV = 4
DN_CHUNK = 128
MLA_ROPE = 64
MLA_QK = 192
MLA_Q_RANK = 256
MLA_KV_RANK = 128
MLA_QPAD = 256
ROPE_THETA = 10000.0
LOG2E = 1.4426950408889634
MLA_MAX_BOUND = 48.0
EXP2_UNDERFLOW = 150.0
VMEM_LIMIT = 56 * 1024 * 1024
FFN_ROWS = 512


def _params(*sem):
    return pltpu.CompilerParams(dimension_semantics=sem, vmem_limit_bytes=VMEM_LIMIT)


def _rms(x):
    return x * lax.rsqrt(jnp.mean(x * x, axis=-1, keepdims=True) + NORM_EPS)


def _sigmoid(x):
    return 1.0 / (1.0 + jnp.exp(-x))


def _silu(x):
    return x / (1.0 + jnp.exp2(x * (-LOG2E)))


def _softplus(x):
    return jnp.maximum(x, 0.0) + jnp.log(1.0 + jnp.exp(-jnp.abs(x)))


def _softplus2(x):
    return jnp.maximum(x, 0.0) + jnp.log(1.0 + jnp.exp2(-jnp.abs(x))) * LOG2E


def _dot(a, b):
    return jnp.dot(a, b, preferred_element_type=F32)


def _dot_nt(a, b):
    return lax.dot_general(a, b, (((1,), (1,)), ((), ())), preferred_element_type=F32)


def _split3(x):
    hi = x.astype(BF16)
    r = x - hi.astype(F32)
    mid = r.astype(BF16)
    lo = (r - mid.astype(F32)).astype(BF16)
    return hi, mid, lo


def _plain_epilogue(acc, j, extra, o_ref):
    del j, extra
    o_ref[...] = acc.astype(o_ref.dtype)


def _sb_epilogue(acc, j, extra, o_ref):
    qg_ref, kg_ref = extra

    def head_norm(gain, scale):
        for h in range(acc.shape[1] // HEAD_DIM):
            sl = slice(h * HEAD_DIM, (h + 1) * HEAD_DIM)
            o_ref[:, sl] = (_rms(acc[:, sl]) * gain * scale).astype(o_ref.dtype)

    @pl.when(j == 0)
    def _():
        head_norm(qg_ref[...], HEAD_DIM ** -0.5 * LOG2E)

    @pl.when(j == 1)
    def _():
        head_norm(kg_ref[...], 1.0)

    @pl.when(j == 2)
    def _():
        o_ref[...] = acc.astype(o_ref.dtype)


def _norm_matmul_body(x_ref, g_ref, w_ref, *rest, epilogue, n_extra):
    extra, o_ref, h_ref = rest[:n_extra], rest[n_extra], rest[n_extra + 1]
    j = pl.program_id(1)

    @pl.when(j == 0)
    def _():
        h_ref[...] = (_rms(x_ref[...]) * g_ref[...]).astype(BF16)

    epilogue(_dot(h_ref[...], w_ref[...]), j, extra, o_ref)


def _norm_matmul(x, gain, w, *, tm, tn, out_dtype, epilogue=_plain_epilogue, extra=()):
    m, d = x.shape
    n = w.shape[1]
    tm = min(tm, m)
    extra_specs = [pl.BlockSpec(e.shape, lambda i, j: (0, 0)) for e in extra]
    return pl.pallas_call(
        functools.partial(_norm_matmul_body, epilogue=epilogue, n_extra=len(extra)),
        grid=(m // tm, n // tn),
        in_specs=[pl.BlockSpec((tm, d), lambda i, j: (i, 0)),
                  pl.BlockSpec((1, d), lambda i, j: (0, 0)),
                  pl.BlockSpec((d, tn), lambda i, j: (0, j))] + extra_specs,
        out_specs=pl.BlockSpec((tm, tn), lambda i, j: (i, j)),
        out_shape=jax.ShapeDtypeStruct((m, n), out_dtype),
        scratch_shapes=[pltpu.VMEM((tm, d), BF16)],
        compiler_params=_params("parallel", "arbitrary"),
        name="norm_matmul",
    )(x, gain.reshape(1, d), w, *extra)


def _proj_ffn_body(a_ref, wo_ref, x_ref, g_ref, wg_ref, wu_ref, wd_ref, o_ref, h_ref):
    @pl.when(pl.program_id(1) == 0)
    def _():
        y = x_ref[...] + _dot(a_ref[...], wo_ref[...])
        h_ref[...] = (_rms(y) * g_ref[...]).astype(BF16)
        o_ref[...] = y

    h = h_ref[...]
    gate = _dot(h, wg_ref[...])
    up = _dot(h, wu_ref[...])
    act = (_silu(gate) * up).astype(BF16)
    o_ref[...] += _dot(act, wd_ref[...])


def _proj_ffn(a, w_out, x, gain, w_gate_up, w_down, *, tm, th):
    m, d = x.shape
    ka = a.shape[1]
    hidden = w_down.shape[0]
    nh = hidden // th
    tm = min(tm, m)
    once = dict(pipeline_mode=pl.Buffered(1))
    resident = once if nh == 1 else {}
    return pl.pallas_call(
        _proj_ffn_body,
        grid=(m // tm, nh),
        in_specs=[pl.BlockSpec((tm, ka), lambda i, j: (i, 0)),
                  pl.BlockSpec((ka, d), lambda i, j: (0, 0), **once),
                  pl.BlockSpec((tm, d), lambda i, j: (i, 0)),
                  pl.BlockSpec((1, d), lambda i, j: (0, 0)),
                  pl.BlockSpec((d, th), lambda i, j: (0, j), **resident),
                  pl.BlockSpec((d, th), lambda i, j: (0, nh + j), **resident),
                  pl.BlockSpec((th, d), lambda i, j: (j, 0), **resident)],
        out_specs=pl.BlockSpec((tm, d), lambda i, j: (i, 0)),
        out_shape=jax.ShapeDtypeStruct((m, d), F32),
        scratch_shapes=[pltpu.VMEM((tm, d), BF16)],
        compiler_params=_params("parallel", "arbitrary"),
        name="proj_ffn",
    )(a, w_out, x, gain.reshape(1, d), w_gate_up, w_gate_up, w_down)


def _sb_attn_body(qg_ref, kg_ref, q_ref, k_ref, v_ref, o_ref, *, tile, nsub, kunroll):
    qi = pl.program_id(2)
    qs = [q_ref[s * tile:(s + 1) * tile, :] for s in range(nsub)]
    row = lax.broadcasted_iota(jnp.int32, (tile, tile), 0)
    col = lax.broadcasted_iota(jnp.int32, (tile, tile), 1)
    suffix_ones = (row >= col).astype(BF16)
    suffix_ones2 = jnp.concatenate([suffix_ones, suffix_ones], axis=0)
    past = col < row

    def sweep(starts, subs, stays, accs, diag_sub):
        ks = [k_ref[pl.ds(st, tile), :] for st in starts]
        vs = [v_ref[pl.ds(st, tile), :] for st in starts]
        jobs = [(t, s) for t in range(len(starts)) for s in subs]
        zs = [_dot_nt(qs[s], ks[t]) for t, s in jobs]
        sps = [_softplus2(z) for z in zs]
        sps = [jnp.where(past, sp, 0.0) if s == diag_sub else sp for (_, s), sp in zip(jobs, sps)]
        his = [sp.astype(BF16) for sp in sps]
        los = [(sp - hi.astype(F32)).astype(BF16) for sp, hi in zip(sps, his)]
        cums = [_dot(jnp.concatenate([hi, lo], axis=1), suffix_ones2) for hi, lo in zip(his, los)]
        stays, accs = list(stays), list(accs)
        ps = []
        for (_, s), z, cum in zip(jobs, zs, cums):
            p = jnp.exp2(z - cum - stays[s])
            ps.append(jnp.where(past, p, 0.0) if s == diag_sub else p)
            stays[s] = stays[s] + cum[:, 0:1]
        for (t, s), p in zip(jobs, ps):
            accs[s] = accs[s] + _dot(p.astype(BF16), vs[t])
        return stays, accs

    stays = [jnp.zeros((tile, 1), F32)] * nsub
    accs = [jnp.zeros((tile, HEAD_DIM), F32)] * nsub
    base = qi * nsub
    for j in reversed(range(nsub)):
        start = pl.multiple_of((base + j) * tile, tile)
        stays, accs = sweep([start], list(range(j, nsub)), stays, accs, j)

    gmax = lambda r: jnp.max(jnp.abs(r[...]), axis=-1, keepdims=True)
    reach = 1.02 * (HEAD_DIM ** 0.5 * LOG2E) * gmax(qg_ref) * gmax(kg_ref) + EXP2_UNDERFLOW

    def body(carry):
        i, st, ac = carry[0], carry[2:2 + nsub], carry[2 + nsub:]
        first = base - 1 - i * kunroll
        starts = [pl.multiple_of((first - u) * tile, tile) for u in range(kunroll)]
        st, ac = sweep(starts, list(range(nsub)), st, ac, None)
        slack = functools.reduce(jnp.minimum, st) - reach
        saturated = (jnp.min(slack) >= 0.0).astype(jnp.int32)
        return (i + 1, saturated) + tuple(st) + tuple(ac)

    def unfinished(carry):
        return jnp.logical_and(carry[0] < base // kunroll, carry[1] == 0)

    carry = lax.while_loop(unfinished, body, (jnp.int32(0), jnp.int32(0)) + tuple(stays) + tuple(accs))
    for s in range(nsub):
        o_ref[s * tile:(s + 1) * tile, :] = carry[2 + nsub + s].astype(o_ref.dtype)


def _sb_attention(qkv, q_gain, k_gain, *, batch, seq, tile, nsub, kunroll):
    m = qkv.shape[0]
    nq = seq // (tile * nsub)
    assert nsub % kunroll == 0
    gain_spec = pl.BlockSpec((1, HEAD_DIM), lambda b, h, i: (0, 0))
    return pl.pallas_call(
        functools.partial(_sb_attn_body, tile=tile, nsub=nsub, kunroll=kunroll),
        grid=(batch, N_HEADS, nq),
        in_specs=[gain_spec, gain_spec,
                  pl.BlockSpec((tile * nsub, HEAD_DIM), lambda b, h, i: (b * nq + i, h)),
                  pl.BlockSpec((seq, HEAD_DIM), lambda b, h, i: (b, N_HEADS + h)),
                  pl.BlockSpec((seq, HEAD_DIM), lambda b, h, i: (b, 2 * N_HEADS + h))],
        out_specs=pl.BlockSpec((tile * nsub, HEAD_DIM), lambda b, h, i: (b * nq + i, h)),
        out_shape=jax.ShapeDtypeStruct((m, N_HEADS * HEAD_DIM), BF16),
        compiler_params=_params("parallel", "parallel", "arbitrary"),
        name="sb_attention",
    )(q_gain, k_gain, qkv, qkv, qkv)


def _rope(x, cos_t, sin_a, sin_b):
    return x * cos_t + pltpu.roll(x, 96, 1) * sin_a + pltpu.roll(x, 32, 1) * sin_b


def _rms_rope_part(x, gain):
    ms = jnp.sum(x * x, axis=-1, keepdims=True) * (1.0 / MLA_ROPE)
    return x * lax.rsqrt(ms + NORM_EPS) * gain


def _mla_score_scale():
    return MLA_QK ** -0.5 * LOG2E


def _mla_score_bound(q_nope_g, q_rope_g, k_nope_g, k_rope_g):
    gmax = lambda g: jnp.max(jnp.abs(g), axis=-1, keepdims=True)
    return 1.02 * _mla_score_scale() * (HEAD_DIM * gmax(q_nope_g) * gmax(k_nope_g)
                                         + MLA_ROPE * gmax(q_rope_g) * gmax(k_rope_g))


def _mla_prep_body(d_ref, wq_ref, wkv_ref, qa_ref, kva_ref, qn_ref, qr_ref, kn_ref, kr_ref,
                   cos_ref, sa_ref, sb_ref, q_out, kn_out, v_out, kr_out):
    d = d_ref[...]
    cos_t, sin_a, sin_b = cos_ref[...], sa_ref[...], sb_ref[...]
    scale = _mla_score_scale()
    lane = lax.broadcasted_iota(jnp.int32, (1, LANES), 1)
    shift = _mla_score_bound(qn_ref[...], qr_ref[...], kn_ref[...], kr_ref[...])
    q_bias = jnp.where(lane == MLA_ROPE, -shift, 0.0)
    k_bias = jnp.where(lane == MLA_ROPE, 1.0, 0.0)
    cq = (_rms(d[:, :MLA_Q_RANK]) * qa_ref[...]).astype(BF16)
    ckv = (_rms(d[:, MLA_Q_RANK:MLA_Q_RANK + MLA_KV_RANK]) * kva_ref[...]).astype(BF16)
    qp = _dot(cq, wq_ref[...])
    kv = _dot(ckv, wkv_ref[...])
    for h in range(N_HEADS):
        c0 = h * MLA_QPAD
        nope = _rms(qp[:, c0:c0 + HEAD_DIM]) * qn_ref[...]
        q_out[:, c0:c0 + HEAD_DIM] = (nope * scale).astype(BF16)
        rot = _rope(_rms_rope_part(qp[:, c0 + HEAD_DIM:c0 + MLA_QPAD], qr_ref[...]), cos_t, sin_a, sin_b)
        q_out[:, c0 + HEAD_DIM:c0 + MLA_QPAD] = (rot * scale + q_bias).astype(BF16)
        ks = slice(h * HEAD_DIM, (h + 1) * HEAD_DIM)
        kn_out[:, ks] = (_rms(kv[:, ks]) * kn_ref[...]).astype(BF16)
    v_out[...] = kv[:, N_HEADS * HEAD_DIM:].astype(BF16)
    k_rope = d[:, MLA_Q_RANK + MLA_KV_RANK:]
    kr_out[...] = (_rope(_rms_rope_part(k_rope, kr_ref[...]), cos_t, sin_a, sin_b) + k_bias).astype(BF16)


def _mla_prep(down, wq, wkv, gains, tables, *, seq, tm):
    m = down.shape[0]
    nt = seq // tm
    full = lambda a: pl.BlockSpec(a.shape, lambda i: (0, 0))
    tab = pl.BlockSpec((tm, LANES), lambda i: (i % nt, 0))
    row = lambda n: pl.BlockSpec((tm, n), lambda i: (i, 0))
    return pl.pallas_call(
        _mla_prep_body,
        grid=(m // tm,),
        in_specs=[row(down.shape[1]), full(wq), full(wkv)] + [full(g) for g in gains] + [tab, tab, tab],
        out_specs=[row(N_HEADS * MLA_QPAD), row(N_HEADS * HEAD_DIM), row(N_HEADS * HEAD_DIM), row(LANES)],
        out_shape=[jax.ShapeDtypeStruct((m, N_HEADS * MLA_QPAD), BF16),
                   jax.ShapeDtypeStruct((m, N_HEADS * HEAD_DIM), BF16),
                   jax.ShapeDtypeStruct((m, N_HEADS * HEAD_DIM), BF16),
                   jax.ShapeDtypeStruct((m, LANES), BF16)],
        compiler_params=_params("parallel"),
        name="mla_prep",
    )(down, wq, wkv, *gains, *tables)


def _mla_attn_body(q_ref, kn_ref, kr_ref, v_ref, o_ref, *, tile, nsub):
    qi = pl.program_id(2)
    qs = [q_ref[s * tile:(s + 1) * tile, :] for s in range(nsub)]
    row = lax.broadcasted_iota(jnp.int32, (tile, tile), 0)
    col = lax.broadcasted_iota(jnp.int32, (tile, tile), 1)
    visible = col <= row

    def sweep(start, subs, carry, diag_sub):
        k = jnp.concatenate([kn_ref[pl.ds(start, tile), :], kr_ref[pl.ds(start, tile), :]], axis=1)
        v = v_ref[pl.ds(start, tile), :]
        ss = [_dot_nt(qs[s], k) for s in subs]
        ss = [jnp.where(visible, x, -1e30) if s == diag_sub else x for s, x in zip(subs, ss)]
        m_new = [jnp.maximum(carry[s][0], jnp.max(x, axis=-1, keepdims=True)) for s, x in zip(subs, ss)]
        ps = [jnp.exp2(x - mn) for x, mn in zip(ss, m_new)]
        carry = list(carry)
        for s, p, mn in zip(subs, ps, m_new):
            m_run, l_run, acc = carry[s]
            alpha = jnp.exp2(m_run - mn)
            carry[s] = (mn, alpha * l_run + jnp.sum(p, axis=-1, keepdims=True),
                        alpha * acc + _dot(p.astype(BF16), v))
        return carry

    def flat(carry):
        return tuple(x for c in carry for x in c)

    def body(i, fc):
        carry = [tuple(fc[3 * s:3 * s + 3]) for s in range(nsub)]
        return flat(sweep(pl.multiple_of(i * tile, tile), list(range(nsub)), carry, None))

    base = qi * nsub
    init = [(jnp.full((tile, 1), -1e30, F32), jnp.zeros((tile, 1), F32),
             jnp.zeros((tile, HEAD_DIM), F32))] * nsub
    fc = lax.fori_loop(0, base, body, flat(init))
    carry = [tuple(fc[3 * s:3 * s + 3]) for s in range(nsub)]
    for j in range(nsub):
        start = pl.multiple_of((base + j) * tile, tile)
        carry = sweep(start, list(range(j, nsub)), carry, j)
    for s in range(nsub):
        o_ref[s * tile:(s + 1) * tile, :] = (carry[s][2] / carry[s][1]).astype(o_ref.dtype)


def _mla_attn_bounded_body(q_ref, kn_ref, kr_ref, v_ref, o_ref, *, tile, nsub, kunroll):
    qi = pl.program_id(2)
    qs = [q_ref[s * tile:(s + 1) * tile, :] for s in range(nsub)]
    row = lax.broadcasted_iota(jnp.int32, (tile, tile), 0)
    col = lax.broadcasted_iota(jnp.int32, (tile, tile), 1)
    visible = col <= row
    ones_col = (lax.broadcasted_iota(jnp.int32, (tile, LANES), 1) == 0).astype(BF16)

    def sweep(starts, subs, accs, diag_sub):
        ks = [jnp.concatenate([kn_ref[pl.ds(st, tile), :], kr_ref[pl.ds(st, tile), :]], axis=1) for st in starts]
        vs = [jnp.concatenate([v_ref[pl.ds(st, tile), :], ones_col], axis=1) for st in starts]
        jobs = [(t, s) for t in range(len(starts)) for s in subs]
        ss = [_dot_nt(qs[s], ks[t]) for t, s in jobs]
        ss = [jnp.where(visible, x, -1e30) if s == diag_sub else x for (_, s), x in zip(jobs, ss)]
        ps = [jnp.exp2(x).astype(BF16) for x in ss]
        accs = list(accs)
        for (t, s), p in zip(jobs, ps):
            accs[s] = accs[s] + _dot(p, vs[t])
        return accs

    def body(i, accs):
        starts = [pl.multiple_of((i * kunroll + u) * tile, tile) for u in range(kunroll)]
        return tuple(sweep(starts, list(range(nsub)), accs, None))

    base = qi * nsub
    accs = lax.fori_loop(0, base // kunroll, body, (jnp.zeros((tile, 2 * HEAD_DIM), F32),) * nsub)
    for j in range(nsub):
        start = pl.multiple_of((base + j) * tile, tile)
        accs = sweep([start], list(range(j, nsub)), accs, j)
    for s in range(nsub):
        o_ref[s * tile:(s + 1) * tile, :] = (
            accs[s][:, :HEAD_DIM] / accs[s][:, HEAD_DIM:HEAD_DIM + 1]).astype(o_ref.dtype)


def _mla_attention(q, kn, kr, v, *, bounded, batch, seq, tile, nsub, kunroll):
    m = q.shape[0]
    nq = seq // (tile * nsub)
    assert nsub % kunroll == 0
    body = (functools.partial(_mla_attn_bounded_body, kunroll=kunroll) if bounded else _mla_attn_body)
    return pl.pallas_call(
        functools.partial(body, tile=tile, nsub=nsub),
        grid=(batch, N_HEADS, nq),
        in_specs=[pl.BlockSpec((tile * nsub, MLA_QPAD), lambda b, h, i: (b * nq + i, h)),
                  pl.BlockSpec((seq, HEAD_DIM), lambda b, h, i: (b, h)),
                  pl.BlockSpec((seq, LANES), lambda b, h, i: (b, 0)),
                  pl.BlockSpec((seq, HEAD_DIM), lambda b, h, i: (b, h))],
        out_specs=pl.BlockSpec((tile * nsub, HEAD_DIM), lambda b, h, i: (b * nq + i, h)),
        out_shape=jax.ShapeDtypeStruct((m, N_HEADS * HEAD_DIM), BF16),
        compiler_params=_params("parallel", "parallel", "arbitrary"),
        name="mla_attention_bounded" if bounded else "mla_attention_online",
    )(q, kn, kr, v)


def _dn_body(x_ref, ab_ref, cw_ref, alog_ref, dtb_ref, onorm_ref, o_ref, state_ref, ext_ref):
    c = DN_CHUNK
    width = N_HEADS * HEAD_DIM

    @pl.when(pl.program_id(1) == 0)
    def _():
        state_ref[...] = jnp.zeros_like(state_ref)
        ext_ref[0:8, :] = jnp.zeros((8, 3 * width), F32)

    ext_ref[8:8 + c, :] = x_ref[:, 0:3 * width].astype(F32)

    def conv_silu(col):
        sl = slice(col, col + HEAD_DIM)
        y = cw_ref[DN_CONV - 1:DN_CONV, sl] * ext_ref[8:8 + c, sl]
        for j in range(DN_CONV - 1):
            y = y + cw_ref[j:j + 1, sl] * ext_ref[5 + j:5 + j + c, sl]
        return _silu(y)

    ab = ab_ref[...]
    g_all = -jnp.exp(alog_ref[...]) * _softplus(ab + dtb_ref[...])
    beta_all = _sigmoid(ab)
    row = lax.broadcasted_iota(jnp.int32, (c, c), 0)
    col = lax.broadcasted_iota(jnp.int32, (c, c), 1)
    causal = row >= col
    strict = row > col
    eye = (row == col).astype(F32)
    g_parts = _split3(g_all)
    prefix_ones = causal.astype(BF16)
    all_ones = jnp.ones((c, c), BF16)
    gc_all = sum(_dot(prefix_ones, p) for p in g_parts)
    gt_all = sum(_dot(all_ones, p) for p in g_parts)
    gc_rows = gc_all.T
    levels = []
    b = 1
    while b < c:
        shift = b.bit_length() - 1
        levels.append(((row >> (shift + 1)) == (col >> (shift + 1)))
                      & (((row >> shift) & 1) == 1) & (((col >> shift) & 1) == 0))
        b *= 2

    heads = range(N_HEADS)
    q, k, v = [], [], []
    for h in heads:
        qh = conv_silu(h * HEAD_DIM)
        kh = conv_silu(width + h * HEAD_DIM)
        q.append(qh * lax.rsqrt(jnp.sum(qh * qh, axis=-1, keepdims=True) + NORM_EPS) * (HEAD_DIM ** -0.5))
        k.append(kh * lax.rsqrt(jnp.sum(kh * kh, axis=-1, keepdims=True) + NORM_EPS))
        v.append(conv_silu(2 * width + h * HEAD_DIM))
    g_col = [gc_all[:, h:h + 1] for h in heads]
    g_tot = [gt_all[:, h:h + 1] for h in heads]
    beta = [beta_all[:, N_HEADS + h:N_HEADS + h + 1] for h in heads]
    decay = [jnp.exp(jnp.where(causal, g_col[h] - gc_rows[h:h + 1, :], 0.0)) for h in heads]
    k_bf = [k[h].astype(BF16) for h in heads]
    k_beta = [k[h] * beta[h] for h in heads]
    kk = [_dot_nt(k_beta[h].astype(BF16), k_bf[h]) for h in heads]
    qk = [_dot_nt(q[h].astype(BF16), k_bf[h]) for h in heads]
    lower = [jnp.where(strict, kk[h] * decay[h], 0.0) for h in heads]
    inv = [eye - jnp.where(levels[0], lower[h], 0.0) for h in heads]
    lower_bf = [lower[h].astype(BF16) for h in heads]
    for mask in levels[1:]:
        mask_bf = mask.astype(BF16)
        inv_bf = [inv[h].astype(BF16) for h in heads]
        step = [_dot(inv_bf[h], lower_bf[h] * mask_bf) for h in heads]
        inv = [inv[h] - _dot(step[h].astype(BF16), inv_bf[h]) for h in heads]
    exp_g = [jnp.exp(g_col[h]) for h in heads]
    uw = [_dot(inv[h].astype(BF16),
               jnp.concatenate([v[h] * beta[h], k_beta[h] * exp_g[h]], axis=1).astype(BF16)) for h in heads]
    state = [state_ref[h] for h in heads]
    ws_qs = [_dot(jnp.concatenate([uw[h][:, HEAD_DIM:], q[h] * exp_g[h]], axis=0).astype(BF16),
                  state[h].astype(BF16)) for h in heads]
    v_new = [(uw[h][:, :HEAD_DIM] - ws_qs[h][:c]).astype(BF16) for h in heads]
    attn = [jnp.where(causal, qk[h] * decay[h], 0.0).astype(BF16) for h in heads]
    out = [ws_qs[h][c:] + _dot(attn[h], v_new[h]) for h in heads]
    for h in heads:
        k_dec = k[h] * jnp.exp(g_tot[h] - g_col[h])
        state_ref[h] = state[h] * jnp.exp(g_tot[h]) + _dot(k_dec.T.astype(BF16), v_new[h])
    for h in heads:
        z = x_ref[:, 3 * width + h * HEAD_DIM:3 * width + (h + 1) * HEAD_DIM].astype(F32)
        o_ref[:, h * HEAD_DIM:(h + 1) * HEAD_DIM] = (
            _rms(out[h]) * onorm_ref[...] * _silu(z)).astype(o_ref.dtype)

    ext_ref[0:8, :] = ext_ref[c:c + 8, :]


def _dn_core(qkvz, ab, conv_w, a_log, dt_bias, out_norm, *, batch, seq):
    m = qkvz.shape[0]
    nc = seq // DN_CHUNK
    width = N_HEADS * HEAD_DIM
    full = lambda a: pl.BlockSpec(a.shape, lambda b, t: (0, 0))
    return pl.pallas_call(
        _dn_body,
        grid=(batch, nc),
        in_specs=[pl.BlockSpec((DN_CHUNK, 4 * width), lambda b, t: (b * nc + t, 0)),
                  pl.BlockSpec((DN_CHUNK, LANES), lambda b, t: (b * nc + t, 0)),
                  full(conv_w), full(a_log), full(dt_bias), full(out_norm)],
        out_specs=pl.BlockSpec((DN_CHUNK, width), lambda b, t: (b * nc + t, 0)),
        out_shape=jax.ShapeDtypeStruct((m, width), BF16),
        scratch_shapes=[pltpu.VMEM((N_HEADS, HEAD_DIM, HEAD_DIM), F32),
                        pltpu.VMEM((DN_CHUNK + 8, 3 * width), F32)],
        compiler_params=_params("parallel", "arbitrary"),
        name="dn_core",
    )(qkvz, ab, conv_w, a_log, dt_bias, out_norm)


def _pad_lanes(vec, n=LANES):
    return jnp.pad(vec.astype(F32), (0, n - vec.shape[0])).reshape(1, n)


def _deltanet_mixer(x, mix_norm, w_in, conv_w, a_log, dt_bias, out_norm, *, batch, seq):
    width = N_HEADS * HEAD_DIM
    w_main = w_in[:, :4 * width].astype(BF16)
    w_gate = jnp.pad(w_in[:, 4 * width:], ((0, 0), (0, LANES - 2 * N_HEADS))).astype(BF16)
    qkvz = _norm_matmul(x, mix_norm, w_main, tm=1024, tn=1024, out_dtype=BF16)
    ab = _norm_matmul(x, mix_norm, w_gate, tm=1024, tn=LANES, out_dtype=F32)
    o = _dn_core(qkvz, ab, conv_w, _pad_lanes(a_log), _pad_lanes(dt_bias), out_norm.reshape(1, HEAD_DIM),
                 batch=batch, seq=seq)
    return o


def _stick_breaking_mixer(x, mix_norm, w_qkv, q_norm, k_norm, *, batch, seq):
    gains = (q_norm.reshape(1, HEAD_DIM), k_norm.reshape(1, HEAD_DIM))
    qkv = _norm_matmul(x, mix_norm, w_qkv.astype(BF16), tm=1024, tn=N_HEADS * HEAD_DIM, out_dtype=BF16,
                       epilogue=_sb_epilogue, extra=gains)
    o = _sb_attention(qkv, *gains, batch=batch, seq=seq, tile=256, nsub=4, kunroll=2)
    return o


def _rope_tables(seq):
    half = MLA_ROPE // 2
    inv_freq = ROPE_THETA ** (-jnp.arange(0, MLA_ROPE, 2, dtype=F32) / MLA_ROPE)
    ang = jnp.arange(seq, dtype=F32)[:, None] * inv_freq[None, :]
    cos, sin, zero = jnp.cos(ang), jnp.sin(ang), jnp.zeros((seq, half), F32)
    pad = jnp.zeros((seq, LANES - MLA_ROPE), F32)
    return (jnp.concatenate([cos, cos, pad], axis=1),
            jnp.concatenate([-sin, zero, pad], axis=1),
            jnp.concatenate([zero, sin, pad], axis=1))


def _mla_mixer(x, mix_norm, w_down, q_a_norm, kv_a_norm, w_uq, w_ukv, q_nope_norm, q_rope_norm,
               k_nope_norm, k_rope_norm, *, batch, seq):
    down_w = jnp.pad(w_down, ((0, 0), (0, 512 - w_down.shape[1]))).astype(BF16)
    wq = jnp.pad(w_uq.reshape(MLA_Q_RANK, N_HEADS, MLA_QK),
                 ((0, 0), (0, 0), (0, MLA_QPAD - MLA_QK))).reshape(MLA_Q_RANK, N_HEADS * MLA_QPAD).astype(BF16)
    wkv = w_ukv.reshape(MLA_KV_RANK, N_HEADS, 2, HEAD_DIM).transpose(0, 2, 1, 3).reshape(
        MLA_KV_RANK, 2 * N_HEADS * HEAD_DIM).astype(BF16)
    down = _norm_matmul(x, mix_norm, down_w, tm=1024, tn=512, out_dtype=F32)
    gains = (q_a_norm.reshape(1, -1), kv_a_norm.reshape(1, -1), q_nope_norm.reshape(1, -1),
             _pad_lanes(q_rope_norm), k_nope_norm.reshape(1, -1), _pad_lanes(k_rope_norm))
    q, kn, v, kr = _mla_prep(down, wq, wkv, gains, _rope_tables(seq), seq=seq, tm=512)
    attend = functools.partial(_mla_attention, q, kn, kr, v, batch=batch, seq=seq, tile=256, nsub=4,
                               kunroll=2)
    bound = _mla_score_bound(gains[2], gains[3], gains[4], gains[5])[0, 0]
    o = lax.cond(bound < MLA_MAX_BOUND, lambda: attend(bounded=True), lambda: attend(bounded=False))
    return o


def kernel(x, l0_mix_norm, l0_dn_w_in, l0_dn_conv_w, l0_dn_a_log, l0_dn_dt_bias, l0_dn_out_norm, l0_dn_w_out, l0_ffn_norm, l0_ffn_w_gate_up, l0_ffn_w_down, l1_mix_norm, l1_sb_w_qkv, l1_sb_q_norm, l1_sb_k_norm, l1_sb_w_out, l1_ffn_norm, l1_ffn_w_gate_up, l1_ffn_w_down, l2_mix_norm, l2_mla_w_down, l2_mla_q_a_norm, l2_mla_kv_a_norm, l2_mla_w_uq, l2_mla_w_ukv, l2_mla_q_nope_norm, l2_mla_q_rope_norm, l2_mla_k_nope_norm, l2_mla_k_rope_norm, l2_mla_w_out, l2_ffn_norm, l2_ffn_w_gate_up, l2_ffn_w_down, l3_mix_norm, l3_dn_w_in, l3_dn_conv_w, l3_dn_a_log, l3_dn_dt_bias, l3_dn_out_norm, l3_dn_w_out, l3_ffn_norm, l3_ffn_w_gate_up, l3_ffn_w_down):
    batch, seq, d_model = x.shape
    shape = dict(batch=batch, seq=seq)
    def close_layer(mixed, w_out, y, g, wgu, wd):
        return _proj_ffn(mixed, w_out.astype(BF16), y, g, wgu.astype(BF16), wd.astype(BF16),
                         tm=FFN_ROWS, th=wd.shape[0])

    y = x.reshape(batch * seq, d_model)
    o = _deltanet_mixer(y, l0_mix_norm, l0_dn_w_in, l0_dn_conv_w, l0_dn_a_log, l0_dn_dt_bias, l0_dn_out_norm,
                        **shape)
    y = close_layer(o, l0_dn_w_out, y, l0_ffn_norm, l0_ffn_w_gate_up, l0_ffn_w_down)
    o = _stick_breaking_mixer(y, l1_mix_norm, l1_sb_w_qkv, l1_sb_q_norm, l1_sb_k_norm, **shape)
    y = close_layer(o, l1_sb_w_out, y, l1_ffn_norm, l1_ffn_w_gate_up, l1_ffn_w_down)
    o = _mla_mixer(y, l2_mix_norm, l2_mla_w_down, l2_mla_q_a_norm, l2_mla_kv_a_norm, l2_mla_w_uq, l2_mla_w_ukv,
                   l2_mla_q_nope_norm, l2_mla_q_rope_norm, l2_mla_k_nope_norm, l2_mla_k_rope_norm, **shape)
    y = close_layer(o, l2_mla_w_out, y, l2_ffn_norm, l2_ffn_w_gate_up, l2_ffn_w_down)
    o = _deltanet_mixer(y, l3_mix_norm, l3_dn_w_in, l3_dn_conv_w, l3_dn_a_log, l3_dn_dt_bias, l3_dn_out_norm,
                        **shape)
    y = close_layer(o, l3_dn_w_out, y, l3_ffn_norm, l3_ffn_w_gate_up, l3_ffn_w_down)
    return y.reshape(batch, seq, d_model)
```

```python
import functools

import jax
import jax.numpy as jnp
from jax import lax
from jax.experimental import pallas as pl
from jax.experimental.pallas import tpu as pltpu

F32 = jnp.float32
BF16 = jnp.bfloat16

NORM_EPS = 1e-6
LANES = 128
HEAD_DIM = 128
N_HEADS = 8
DN_CONV = 4
DN_CHUNK = 128
MLA_ROPE = 64
MLA_QK = 192
MLA_Q_RANK = 256
MLA_KV_RANK = 128
MLA_QPAD = 256
ROPE_THETA = 10000.0
LOG2E = 1.4426950408889634
MLA_MAX_BOUND = 48.0
EXP2_UNDERFLOW = 150.0
VMEM_LIMIT = 56 * 1024 * 1024
FFN_ROWS = 512


def _params(*sem):
    return pltpu.CompilerParams(dimension_semantics=sem, vmem_limit_bytes=VMEM_LIMIT)


def _rms(x):
    return x * lax.rsqrt(jnp.mean(x * x, axis=-1, keepdims=True) + NORM_EPS)


def _sigmoid(x):
    return 1.0 / (1.0 + jnp.exp(-x))


def _silu(x):
    return x / (1.0 + jnp.exp2(x * (-LOG2E)))


def _softplus(x):
    return jnp.maximum(x, 0.0) + jnp.log(1.0 + jnp.exp(-jnp.abs(x)))


def _softplus2(x):
    return jnp.maximum(x, 0.0) + jnp.log(1.0 + jnp.exp2(-jnp.abs(x))) * LOG2E


def _dot(a, b):
    return jnp.dot(a, b, preferred_element_type=F32)


def _dot_nt(a, b):
    return lax.dot_general(a, b, (((1,), (1,)), ((), ())), preferred_element_type=F32)


def _split3(x):
    hi = x.astype(BF16)
    r = x - hi.astype(F32)
    mid = r.astype(BF16)
    lo = (r - mid.astype(F32)).astype(BF16)
    return hi, mid, lo


def _plain_epilogue(acc, j, extra, o_ref):
    del j, extra
    o_ref[...] = acc.astype(o_ref.dtype)


def _sb_epilogue(acc, j, extra, o_ref):
    qg_ref, kg_ref = extra

    def head_norm(gain, scale):
        for h in range(acc.shape[1] // HEAD_DIM):
            sl = slice(h * HEAD_DIM, (h + 1) * HEAD_DIM)
            o_ref[:, sl] = (_rms(acc[:, sl]) * gain * scale).astype(o_ref.dtype)

    @pl.when(j == 0)
    def _():
        head_norm(qg_ref[...], HEAD_DIM ** -0.5 * LOG2E)

    @pl.when(j == 1)
    def _():
        head_norm(kg_ref[...], 1.0)

    @pl.when(j == 2)
    def _():
        o_ref[...] = acc.astype(o_ref.dtype)


def _norm_matmul_body(x_ref, g_ref, w_ref, *rest, epilogue, n_extra):
    extra, o_ref, h_ref = rest[:n_extra], rest[n_extra], rest[n_extra + 1]
    j = pl.program_id(1)

    @pl.when(j == 0)
    def _():
        h_ref[...] = (_rms(x_ref[...]) * g_ref[...]).astype(BF16)

    epilogue(_dot(h_ref[...], w_ref[...]), j, extra, o_ref)


def _norm_matmul(x, gain, w, *, tm, tn, out_dtype, epilogue=_plain_epilogue, extra=()):
    m, d = x.shape
    n = w.shape[1]
    tm = min(tm, m)
    extra_specs = [pl.BlockSpec(e.shape, lambda i, j: (0, 0)) for e in extra]
    return pl.pallas_call(
        functools.partial(_norm_matmul_body, epilogue=epilogue, n_extra=len(extra)),
        grid=(m // tm, n // tn),
        in_specs=[pl.BlockSpec((tm, d), lambda i, j: (i, 0)),
                  pl.BlockSpec((1, d), lambda i, j: (0, 0)),
                  pl.BlockSpec((d, tn), lambda i, j: (0, j))] + extra_specs,
        out_specs=pl.BlockSpec((tm, tn), lambda i, j: (i, j)),
        out_shape=jax.ShapeDtypeStruct((m, n), out_dtype),
        scratch_shapes=[pltpu.VMEM((tm, d), BF16)],
        compiler_params=_params("parallel", "arbitrary"),
        name="norm_matmul",
    )(x, gain.reshape(1, d), w, *extra)


def _proj_ffn_body(a_ref, wo_ref, x_ref, g_ref, wg_ref, wu_ref, wd_ref, o_ref, h_ref):
    @pl.when(pl.program_id(1) == 0)
    def _():
        y = x_ref[...] + _dot(a_ref[...], wo_ref[...])
        h_ref[...] = (_rms(y) * g_ref[...]).astype(BF16)
        o_ref[...] = y

    h = h_ref[...]
    gate = _dot(h, wg_ref[...])
    up = _dot(h, wu_ref[...])
    act = (_silu(gate) * up).astype(BF16)
    o_ref[...] += _dot(act, wd_ref[...])


def _proj_ffn(a, w_out, x, gain, w_gate_up, w_down, *, tm, th):
    m, d = x.shape
    ka = a.shape[1]
    hidden = w_down.shape[0]
    nh = hidden // th
    tm = min(tm, m)
    once = dict(pipeline_mode=pl.Buffered(1))
    resident = once if nh == 1 else {}
    return pl.pallas_call(
        _proj_ffn_body,
        grid=(m // tm, nh),
        in_specs=[pl.BlockSpec((tm, ka), lambda i, j: (i, 0)),
                  pl.BlockSpec((ka, d), lambda i, j: (0, 0), **once),
                  pl.BlockSpec((tm, d), lambda i, j: (i, 0)),
                  pl.BlockSpec((1, d), lambda i, j: (0, 0)),
                  pl.BlockSpec((d, th), lambda i, j: (0, j), **resident),
                  pl.BlockSpec((d, th), lambda i, j: (0, nh + j), **resident),
                  pl.BlockSpec((th, d), lambda i, j: (j, 0), **resident)],
        out_specs=pl.BlockSpec((tm, d), lambda i, j: (i, 0)),
        out_shape=jax.ShapeDtypeStruct((m, d), F32),
        scratch_shapes=[pltpu.VMEM((tm, d), BF16)],
        compiler_params=_params("parallel", "arbitrary"),
        name="proj_ffn",
    )(a, w_out, x, gain.reshape(1, d), w_gate_up, w_gate_up, w_down)


def _sb_attn_body(qg_ref, kg_ref, q_ref, k_ref, v_ref, o_ref, *, tile, nsub, kunroll):
    qi = pl.program_id(2)
    qs = [q_ref[s * tile:(s + 1) * tile, :] for s in range(nsub)]
    row = lax.broadcasted_iota(jnp.int32, (tile, tile), 0)
    col = lax.broadcasted_iota(jnp.int32, (tile, tile), 1)
    suffix_ones = (row >= col).astype(BF16)
    suffix_ones2 = jnp.concatenate([suffix_ones, suffix_ones], axis=0)
    past = col < row

    def sweep(starts, jobs, stays, accs):
        ks = [k_ref[pl.ds(st, tile), :] for st in starts]
        vs = [v_ref[pl.ds(st, tile), :] for st in starts]
        zs = [_dot_nt(qs[s], ks[t]) for t, s, _ in jobs]
        sps = [_softplus2(z) for z in zs]
        sps = [jnp.where(past, sp, 0.0) if masked else sp for (_, _, masked), sp in zip(jobs, sps)]
        his = [sp.astype(BF16) for sp in sps]
        los = [(sp - hi.astype(F32)).astype(BF16) for sp, hi in zip(sps, his)]
        cums = [_dot(jnp.concatenate([hi, lo], axis=1), suffix_ones2) for hi, lo in zip(his, los)]
        stays, accs = list(stays), list(accs)
        ps = []
        for (_, s, masked), z, cum in zip(jobs, zs, cums):
            p = jnp.exp2(z - cum - stays[s])
            ps.append(jnp.where(past, p, 0.0) if masked else p)
            stays[s] = stays[s] + cum[:, 0:1]
        for (t, s, _), p in zip(jobs, ps):
            accs[s] = accs[s] + _dot(p.astype(BF16), vs[t])
        return stays, accs

    stays = [jnp.zeros((tile, 1), F32)] * nsub
    accs = [jnp.zeros((tile, HEAD_DIM), F32)] * nsub
    base = qi * nsub
    own = list(reversed(range(nsub)))
    stays, accs = sweep([pl.multiple_of((base + j) * tile, tile) for j in own],
                        [(t, s, s == j) for t, j in enumerate(own) for s in range(j, nsub)], stays, accs)
    full = [(t, s, False) for t in range(kunroll) for s in range(nsub)]

    gmax = lambda r: jnp.max(jnp.abs(r[...]), axis=-1, keepdims=True)
    reach = 1.02 * (HEAD_DIM ** 0.5 * LOG2E) * gmax(qg_ref) * gmax(kg_ref) + EXP2_UNDERFLOW

    def body(carry):
        i, st, ac = carry[0], carry[2:2 + nsub], carry[2 + nsub:]
        first = base - 1 - i * kunroll
        starts = [pl.multiple_of((first - u) * tile, tile) for u in range(kunroll)]
        st, ac = sweep(starts, full, st, ac)
        slack = functools.reduce(jnp.minimum, st) - reach
        saturated = (jnp.min(slack) >= 0.0).astype(jnp.int32)
        return (i + 1, saturated) + tuple(st) + tuple(ac)

    def unfinished(carry):
        return jnp.logical_and(carry[0] < base // kunroll, carry[1] == 0)

    carry = lax.while_loop(unfinished, body, (jnp.int32(0), jnp.int32(0)) + tuple(stays) + tuple(accs))
    for s in range(nsub):
        o_ref[s * tile:(s + 1) * tile, :] = carry[2 + nsub + s].astype(o_ref.dtype)


def _sb_attention(qkv, q_gain, k_gain, *, batch, seq, tile, nsub, kunroll):
    m = qkv.shape[0]
    nq = seq // (tile * nsub)
    assert nsub % kunroll == 0
    gain_spec = pl.BlockSpec((1, HEAD_DIM), lambda b, h, i: (0, 0))
    return pl.pallas_call(
        functools.partial(_sb_attn_body, tile=tile, nsub=nsub, kunroll=kunroll),
        grid=(batch, N_HEADS, nq),
        in_specs=[gain_spec, gain_spec,
                  pl.BlockSpec((tile * nsub, HEAD_DIM), lambda b, h, i: (b * nq + i, h)),
                  pl.BlockSpec((seq, HEAD_DIM), lambda b, h, i: (b, N_HEADS + h)),
                  pl.BlockSpec((seq, HEAD_DIM), lambda b, h, i: (b, 2 * N_HEADS + h))],
        out_specs=pl.BlockSpec((tile * nsub, HEAD_DIM), lambda b, h, i: (b * nq + i, h)),
        out_shape=jax.ShapeDtypeStruct((m, N_HEADS * HEAD_DIM), BF16),
        compiler_params=_params("parallel", "parallel", "arbitrary"),
        name="sb_attention",
    )(q_gain, k_gain, qkv, qkv, qkv)


def _rope(x, cos_t, sin_a, sin_b):
    return x * cos_t + pltpu.roll(x, 96, 1) * sin_a + pltpu.roll(x, 32, 1) * sin_b


def _rms_rope_part(x, gain):
    ms = jnp.sum(x * x, axis=-1, keepdims=True) * (1.0 / MLA_ROPE)
    return x * lax.rsqrt(ms + NORM_EPS) * gain


def _mla_score_scale():
    return MLA_QK ** -0.5 * LOG2E


def _mla_score_bound(q_nope_g, q_rope_g, k_nope_g, k_rope_g):
    gmax = lambda g: jnp.max(jnp.abs(g), axis=-1, keepdims=True)
    return 1.02 * _mla_score_scale() * (HEAD_DIM * gmax(q_nope_g) * gmax(k_nope_g)
                                         + MLA_ROPE * gmax(q_rope_g) * gmax(k_rope_g))


def _mla_prep_body(d_ref, wq_ref, wkv_ref, qa_ref, kva_ref, qn_ref, qr_ref, kn_ref, kr_ref,
                   cos_ref, sa_ref, sb_ref, q_out, kn_out, v_out, kr_out):
    d = d_ref[...]
    cos_t, sin_a, sin_b = cos_ref[...], sa_ref[...], sb_ref[...]
    scale = _mla_score_scale()
    lane = lax.broadcasted_iota(jnp.int32, (1, LANES), 1)
    shift = _mla_score_bound(qn_ref[...], qr_ref[...], kn_ref[...], kr_ref[...])
    q_bias = jnp.where(lane == MLA_ROPE, -shift, 0.0)
    k_bias = jnp.where(lane == MLA_ROPE, 1.0, 0.0)
    cq = (_rms(d[:, :MLA_Q_RANK]) * qa_ref[...]).astype(BF16)
    ckv = (_rms(d[:, MLA_Q_RANK:MLA_Q_RANK + MLA_KV_RANK]) * kva_ref[...]).astype(BF16)
    qp = _dot(cq, wq_ref[...])
    kv = _dot(ckv, wkv_ref[...])
    for h in range(N_HEADS):
        c0 = h * MLA_QPAD
        nope = _rms(qp[:, c0:c0 + HEAD_DIM]) * qn_ref[...]
        q_out[:, c0:c0 + HEAD_DIM] = (nope * scale).astype(BF16)
        rot = _rope(_rms_rope_part(qp[:, c0 + HEAD_DIM:c0 + MLA_QPAD], qr_ref[...]), cos_t, sin_a, sin_b)
        q_out[:, c0 + HEAD_DIM:c0 + MLA_QPAD] = (rot * scale + q_bias).astype(BF16)
        ks = slice(h * HEAD_DIM, (h + 1) * HEAD_DIM)
        kn_out[:, ks] = (_rms(kv[:, ks]) * kn_ref[...]).astype(BF16)
    v_out[...] = kv[:, N_HEADS * HEAD_DIM:].astype(BF16)
    k_rope = d[:, MLA_Q_RANK + MLA_KV_RANK:]
    kr_out[...] = (_rope(_rms_rope_part(k_rope, kr_ref[...]), cos_t, sin_a, sin_b) + k_bias).astype(BF16)


def _mla_prep(down, wq, wkv, gains, tables, *, seq, tm):
    m = down.shape[0]
    nt = seq // tm
    full = lambda a: pl.BlockSpec(a.shape, lambda i: (0, 0))
    tab = pl.BlockSpec((tm, LANES), lambda i: (i % nt, 0))
    row = lambda n: pl.BlockSpec((tm, n), lambda i: (i, 0))
    return pl.pallas_call(
        _mla_prep_body,
        grid=(m // tm,),
        in_specs=[row(down.shape[1]), full(wq), full(wkv)] + [full(g) for g in gains] + [tab, tab, tab],
        out_specs=[row(N_HEADS * MLA_QPAD), row(N_HEADS * HEAD_DIM), row(N_HEADS * HEAD_DIM), row(LANES)],
        out_shape=[jax.ShapeDtypeStruct((m, N_HEADS * MLA_QPAD), BF16),
                   jax.ShapeDtypeStruct((m, N_HEADS * HEAD_DIM), BF16),
                   jax.ShapeDtypeStruct((m, N_HEADS * HEAD_DIM), BF16),
                   jax.ShapeDtypeStruct((m, LANES), BF16)],
        compiler_params=_params("parallel"),
        name="mla_prep",
    )(down, wq, wkv, *gains, *tables)


def _mla_attn_body(q_ref, kn_ref, kr_ref, v_ref, o_ref, *, tile, nsub):
    qi = pl.program_id(2)
    qs = [q_ref[s * tile:(s + 1) * tile, :] for s in range(nsub)]
    row = lax.broadcasted_iota(jnp.int32, (tile, tile), 0)
    col = lax.broadcasted_iota(jnp.int32, (tile, tile), 1)
    visible = col <= row

    def sweep(start, subs, carry, diag_sub):
        k = jnp.concatenate([kn_ref[pl.ds(start, tile), :], kr_ref[pl.ds(start, tile), :]], axis=1)
        v = v_ref[pl.ds(start, tile), :]
        ss = [_dot_nt(qs[s], k) for s in subs]
        ss = [jnp.where(visible, x, -1e30) if s == diag_sub else x for s, x in zip(subs, ss)]
        m_new = [jnp.maximum(carry[s][0], jnp.max(x, axis=-1, keepdims=True)) for s, x in zip(subs, ss)]
        ps = [jnp.exp2(x - mn) for x, mn in zip(ss, m_new)]
        carry = list(carry)
        for s, p, mn in zip(subs, ps, m_new):
            m_run, l_run, acc = carry[s]
            alpha = jnp.exp2(m_run - mn)
            carry[s] = (mn, alpha * l_run + jnp.sum(p, axis=-1, keepdims=True),
                        alpha * acc + _dot(p.astype(BF16), v))
        return carry

    def flat(carry):
        return tuple(x for c in carry for x in c)

    def body(i, fc):
        carry = [tuple(fc[3 * s:3 * s + 3]) for s in range(nsub)]
        return flat(sweep(pl.multiple_of(i * tile, tile), list(range(nsub)), carry, None))

    base = qi * nsub
    init = [(jnp.full((tile, 1), -1e30, F32), jnp.zeros((tile, 1), F32),
             jnp.zeros((tile, HEAD_DIM), F32))] * nsub
    fc = lax.fori_loop(0, base, body, flat(init))
    carry = [tuple(fc[3 * s:3 * s + 3]) for s in range(nsub)]
    for j in range(nsub):
        start = pl.multiple_of((base + j) * tile, tile)
        carry = sweep(start, list(range(j, nsub)), carry, j)
    for s in range(nsub):
        o_ref[s * tile:(s + 1) * tile, :] = (carry[s][2] / carry[s][1]).astype(o_ref.dtype)


def _mla_attn_bounded_body(q_ref, kn_ref, kr_ref, v_ref, o_ref, *, tile, nsub, kunroll):
    qi = pl.program_id(2)
    qs = [q_ref[s * tile:(s + 1) * tile, :] for s in range(nsub)]
    row = lax.broadcasted_iota(jnp.int32, (tile, tile), 0)
    col = lax.broadcasted_iota(jnp.int32, (tile, tile), 1)
    visible = col <= row
    ones_col = (lax.broadcasted_iota(jnp.int32, (tile, LANES), 1) == 0).astype(BF16)

    def sweep(starts, jobs, accs):
        ks = [jnp.concatenate([kn_ref[pl.ds(st, tile), :], kr_ref[pl.ds(st, tile), :]], axis=1) for st in starts]
        vs = [jnp.concatenate([v_ref[pl.ds(st, tile), :], ones_col], axis=1) for st in starts]
        ss = [_dot_nt(qs[s], ks[t]) for t, s, _ in jobs]
        ss = [jnp.where(visible, x, -1e30) if masked else x for (_, _, masked), x in zip(jobs, ss)]
        ps = [jnp.exp2(x).astype(BF16) for x in ss]
        accs = list(accs)
        for (t, s, _), p in zip(jobs, ps):
            accs[s] = accs[s] + _dot(p, vs[t])
        return accs

    full = [(t, s, False) for t in range(kunroll) for s in range(nsub)]

    def body(i, accs):
        starts = [pl.multiple_of((i * kunroll + u) * tile, tile) for u in range(kunroll)]
        return tuple(sweep(starts, full, accs))

    base = qi * nsub
    accs = lax.fori_loop(0, base // kunroll, body, (jnp.zeros((tile, 2 * HEAD_DIM), F32),) * nsub)
    accs = sweep([pl.multiple_of((base + j) * tile, tile) for j in range(nsub)],
                 [(j, s, s == j) for j in range(nsub) for s in range(j, nsub)], accs)
    for s in range(nsub):
        o_ref[s * tile:(s + 1) * tile, :] = (
            accs[s][:, :HEAD_DIM] / accs[s][:, HEAD_DIM:HEAD_DIM + 1]).astype(o_ref.dtype)


def _mla_attention(q, kn, kr, v, *, bounded, batch, seq, tile, nsub, kunroll):
    m = q.shape[0]
    nq = seq // (tile * nsub)
    assert nsub % kunroll == 0
    body = (functools.partial(_mla_attn_bounded_body, kunroll=kunroll) if bounded else _mla_attn_body)
    return pl.pallas_call(
        functools.partial(body, tile=tile, nsub=nsub),
        grid=(batch, N_HEADS, nq),
        in_specs=[pl.BlockSpec((tile * nsub, MLA_QPAD), lambda b, h, i: (b * nq + i, h)),
                  pl.BlockSpec((seq, HEAD_DIM), lambda b, h, i: (b, h)),
                  pl.BlockSpec((seq, LANES), lambda b, h, i: (b, 0)),
                  pl.BlockSpec((seq, HEAD_DIM), lambda b, h, i: (b, h))],
        out_specs=pl.BlockSpec((tile * nsub, HEAD_DIM), lambda b, h, i: (b * nq + i, h)),
        out_shape=jax.ShapeDtypeStruct((m, N_HEADS * HEAD_DIM), BF16),
        compiler_params=_params("parallel", "parallel", "arbitrary"),
        name="mla_attention_bounded" if bounded else "mla_attention_online",
    )(q, kn, kr, v)


def _dn_body(x_ref, ab_ref, cw_ref, alog_ref, dtb_ref, onorm_ref, o_ref, state_ref, ext_ref):
    c = DN_CHUNK
    width = N_HEADS * HEAD_DIM

    @pl.when(pl.program_id(1) == 0)
    def _():
        state_ref[...] = jnp.zeros_like(state_ref)
        ext_ref[0:8, :] = jnp.zeros((8, 3 * width), F32)

    ext_ref[8:8 + c, :] = x_ref[:, 0:3 * width].astype(F32)

    def conv_silu(col):
        sl = slice(col, col + HEAD_DIM)
        y = cw_ref[DN_CONV - 1:DN_CONV, sl] * ext_ref[8:8 + c, sl]
        for j in range(DN_CONV - 1):
            y = y + cw_ref[j:j + 1, sl] * ext_ref[5 + j:5 + j + c, sl]
        return _silu(y)

    ab = ab_ref[...]
    g_all = -jnp.exp(alog_ref[...]) * _softplus(ab + dtb_ref[...])
    beta_all = _sigmoid(ab)
    row = lax.broadcasted_iota(jnp.int32, (c, c), 0)
    col = lax.broadcasted_iota(jnp.int32, (c, c), 1)
    causal = row >= col
    strict = row > col
    eye = (row == col).astype(F32)
    g_parts = _split3(g_all)
    prefix_ones = causal.astype(BF16)
    all_ones = jnp.ones((c, c), BF16)
    gc_all = sum(_dot(prefix_ones, p) for p in g_parts)
    gt_all = sum(_dot(all_ones, p) for p in g_parts)
    gc_rows = gc_all.T
    levels = []
    b = 1
    while b < c:
        shift = b.bit_length() - 1
        levels.append(((row >> (shift + 1)) == (col >> (shift + 1)))
                      & (((row >> shift) & 1) == 1) & (((col >> shift) & 1) == 0))
        b *= 2

    heads = range(N_HEADS)
    q, k, v = [], [], []
    for h in heads:
        qh = conv_silu(h * HEAD_DIM)
        kh = conv_silu(width + h * HEAD_DIM)
        q.append(qh * lax.rsqrt(jnp.sum(qh * qh, axis=-1, keepdims=True) + NORM_EPS) * (HEAD_DIM ** -0.5))
        k.append(kh * lax.rsqrt(jnp.sum(kh * kh, axis=-1, keepdims=True) + NORM_EPS))
        v.append(conv_silu(2 * width + h * HEAD_DIM))
    g_col = [gc_all[:, h:h + 1] for h in heads]
    g_tot = [gt_all[:, h:h + 1] for h in heads]
    beta = [beta_all[:, N_HEADS + h:N_HEADS + h + 1] for h in heads]
    decay = [jnp.exp(jnp.where(causal, g_col[h] - gc_rows[h:h + 1, :], 0.0)) for h in heads]
    k_bf = [k[h].astype(BF16) for h in heads]
    k_beta = [k[h] * beta[h] for h in heads]
    kk = [_dot_nt(k_beta[h].astype(BF16), k_bf[h]) for h in heads]
    qk = [_dot_nt(q[h].astype(BF16), k_bf[h]) for h in heads]
    lower = [jnp.where(strict, kk[h] * decay[h], 0.0) for h in heads]
    inv = [eye - jnp.where(levels[0], lower[h], 0.0) for h in heads]
    lower_bf = [lower[h].astype(BF16) for h in heads]
    for mask in levels[1:]:
        mask_bf = mask.astype(BF16)
        inv_bf = [inv[h].astype(BF16) for h in heads]
        step = [_dot(inv_bf[h], lower_bf[h] * mask_bf) for h in heads]
        inv = [inv[h] - _dot(step[h].astype(BF16), inv_bf[h]) for h in heads]
    exp_g = [jnp.exp(g_col[h]) for h in heads]
    uw = [_dot(inv[h].astype(BF16),
               jnp.concatenate([v[h] * beta[h], k_beta[h] * exp_g[h]], axis=1).astype(BF16)) for h in heads]
    state = [state_ref[h] for h in heads]
    ws_qs = [_dot(jnp.concatenate([uw[h][:, HEAD_DIM:], q[h] * exp_g[h]], axis=0).astype(BF16),
                  state[h].astype(BF16)) for h in heads]
    v_new = [(uw[h][:, :HEAD_DIM] - ws_qs[h][:c]).astype(BF16) for h in heads]
    attn = [jnp.where(causal, qk[h] * decay[h], 0.0).astype(BF16) for h in heads]
    out = [ws_qs[h][c:] + _dot(attn[h], v_new[h]) for h in heads]
    for h in heads:
        k_dec = k[h] * jnp.exp(g_tot[h] - g_col[h])
        state_ref[h] = state[h] * jnp.exp(g_tot[h]) + _dot(k_dec.T.astype(BF16), v_new[h])
    for h in heads:
        z = x_ref[:, 3 * width + h * HEAD_DIM:3 * width + (h + 1) * HEAD_DIM].astype(F32)
        o_ref[:, h * HEAD_DIM:(h + 1) * HEAD_DIM] = (
            _rms(out[h]) * onorm_ref[...] * _silu(z)).astype(o_ref.dtype)

    ext_ref[0:8, :] = ext_ref[c:c + 8, :]


def _dn_core(qkvz, ab, conv_w, a_log, dt_bias, out_norm, *, batch, seq):
    m = qkvz.shape[0]
    nc = seq // DN_CHUNK
    width = N_HEADS * HEAD_DIM
    full = lambda a: pl.BlockSpec(a.shape, lambda b, t: (0, 0))
    return pl.pallas_call(
        _dn_body,
        grid=(batch, nc),
        in_specs=[pl.BlockSpec((DN_CHUNK, 4 * width), lambda b, t: (b * nc + t, 0)),
                  pl.BlockSpec((DN_CHUNK, LANES), lambda b, t: (b * nc + t, 0)),
                  full(conv_w), full(a_log), full(dt_bias), full(out_norm)],
        out_specs=pl.BlockSpec((DN_CHUNK, width), lambda b, t: (b * nc + t, 0)),
        out_shape=jax.ShapeDtypeStruct((m, width), BF16),
        scratch_shapes=[pltpu.VMEM((N_HEADS, HEAD_DIM, HEAD_DIM), F32),
                        pltpu.VMEM((DN_CHUNK + 8, 3 * width), F32)],
        compiler_params=_params("parallel", "arbitrary"),
        name="dn_core",
    )(qkvz, ab, conv_w, a_log, dt_bias, out_norm)


def _pad_lanes(vec, n=LANES):
    return jnp.pad(vec.astype(F32), (0, n - vec.shape[0])).reshape(1, n)


def _deltanet_mixer(x, mix_norm, w_in, conv_w, a_log, dt_bias, out_norm, *, batch, seq):
    width = N_HEADS * HEAD_DIM
    w_main = w_in[:, :4 * width].astype(BF16)
    w_gate = jnp.pad(w_in[:, 4 * width:], ((0, 0), (0, LANES - 2 * N_HEADS))).astype(BF16)
    qkvz = _norm_matmul(x, mix_norm, w_main, tm=1024, tn=1024, out_dtype=BF16)
    ab = _norm_matmul(x, mix_norm, w_gate, tm=1024, tn=LANES, out_dtype=F32)
    o = _dn_core(qkvz, ab, conv_w, _pad_lanes(a_log), _pad_lanes(dt_bias), out_norm.reshape(1, HEAD_DIM),
                 batch=batch, seq=seq)
    return o


def _stick_breaking_mixer(x, mix_norm, w_qkv, q_norm, k_norm, *, batch, seq):
    gains = (q_norm.reshape(1, HEAD_DIM), k_norm.reshape(1, HEAD_DIM))
    qkv = _norm_matmul(x, mix_norm, w_qkv.astype(BF16), tm=1024, tn=N_HEADS * HEAD_DIM, out_dtype=BF16,
                       epilogue=_sb_epilogue, extra=gains)
    o = _sb_attention(qkv, *gains, batch=batch, seq=seq, tile=256, nsub=4, kunroll=1)
    return o


def _rope_tables(seq):
    half = MLA_ROPE // 2
    inv_freq = ROPE_THETA ** (-jnp.arange(0, MLA_ROPE, 2, dtype=F32) / MLA_ROPE)
    ang = jnp.arange(seq, dtype=F32)[:, None] * inv_freq[None, :]
    cos, sin, zero = jnp.cos(ang), jnp.sin(ang), jnp.zeros((seq, half), F32)
    pad = jnp.zeros((seq, LANES - MLA_ROPE), F32)
    return (jnp.concatenate([cos, cos, pad], axis=1),
            jnp.concatenate([-sin, zero, pad], axis=1),
            jnp.concatenate([zero, sin, pad], axis=1))


def _mla_mixer(x, mix_norm, w_down, q_a_norm, kv_a_norm, w_uq, w_ukv, q_nope_norm, q_rope_norm,
               k_nope_norm, k_rope_norm, *, batch, seq):
    down_w = jnp.pad(w_down, ((0, 0), (0, 512 - w_down.shape[1]))).astype(BF16)
    wq = jnp.pad(w_uq.reshape(MLA_Q_RANK, N_HEADS, MLA_QK),
                 ((0, 0), (0, 0), (0, MLA_QPAD - MLA_QK))).reshape(MLA_Q_RANK, N_HEADS * MLA_QPAD).astype(BF16)
    wkv = w_ukv.reshape(MLA_KV_RANK, N_HEADS, 2, HEAD_DIM).transpose(0, 2, 1, 3).reshape(
        MLA_KV_RANK, 2 * N_HEADS * HEAD_DIM).astype(BF16)
    down = _norm_matmul(x, mix_norm, down_w, tm=1024, tn=512, out_dtype=F32)
    gains = (q_a_norm.reshape(1, -1), kv_a_norm.reshape(1, -1), q_nope_norm.reshape(1, -1),
             _pad_lanes(q_rope_norm), k_nope_norm.reshape(1, -1), _pad_lanes(k_rope_norm))
    q, kn, v, kr = _mla_prep(down, wq, wkv, gains, _rope_tables(seq), seq=seq, tm=512)
    attend = functools.partial(_mla_attention, q, kn, kr, v, batch=batch, seq=seq, tile=256, nsub=4,
                               kunroll=4)
    bound = _mla_score_bound(gains[2], gains[3], gains[4], gains[5])[0, 0]
    o = lax.cond(bound < MLA_MAX_BOUND, lambda: attend(bounded=True), lambda: attend(bounded=False))
    return o


def kernel(x, l0_mix_norm, l0_dn_w_in, l0_dn_conv_w, l0_dn_a_log, l0_dn_dt_bias, l0_dn_out_norm, l0_dn_w_out, l0_ffn_norm, l0_ffn_w_gate_up, l0_ffn_w_down, l1_mix_norm, l1_sb_w_qkv, l1_sb_q_norm, l1_sb_k_norm, l1_sb_w_out, l1_ffn_norm, l1_ffn_w_gate_up, l1_ffn_w_down, l2_mix_norm, l2_mla_w_down, l2_mla_q_a_norm, l2_mla_kv_a_norm, l2_mla_w_uq, l2_mla_w_ukv, l2_mla_q_nope_norm, l2_mla_q_rope_norm, l2_mla_k_nope_norm, l2_mla_k_rope_norm, l2_mla_w_out, l2_ffn_norm, l2_ffn_w_gate_up, l2_ffn_w_down, l3_mix_norm, l3_dn_w_in, l3_dn_conv_w, l3_dn_a_log, l3_dn_dt_bias, l3_dn_out_norm, l3_dn_w_out, l3_ffn_norm, l3_ffn_w_gate_up, l3_ffn_w_down):
    batch, seq, d_model = x.shape
    shape = dict(batch=batch, seq=seq)
    def close_layer(mixed, w_out, y, g, wgu, wd):
        return _proj_ffn(mixed, w_out.astype(BF16), y, g, wgu.astype(BF16), wd.astype(BF16),
                         tm=FFN_ROWS, th=wd.shape[0])

    y = x.reshape(batch * seq, d_model)
    o = _deltanet_mixer(y, l0_mix_norm, l0_dn_w_in, l0_dn_conv_w, l0_dn_a_log, l0_dn_dt_bias, l0_dn_out_norm,
                        **shape)
    y = close_layer(o, l0_dn_w_out, y, l0_ffn_norm, l0_ffn_w_gate_up, l0_ffn_w_down)
    o = _stick_breaking_mixer(y, l1_mix_norm, l1_sb_w_qkv, l1_sb_q_norm, l1_sb_k_norm, **shape)
    y = close_layer(o, l1_sb_w_out, y, l1_ffn_norm, l1_ffn_w_gate_up, l1_ffn_w_down)
    o = _mla_mixer(y, l2_mix_norm, l2_mla_w_down, l2_mla_q_a_norm, l2_mla_kv_a_norm, l2_mla_w_uq, l2_mla_w_ukv,
                   l2_mla_q_nope_norm, l2_mla_q_rope_norm, l2_mla_k_nope_norm, l2_mla_k_rope_norm, **shape)
    y = close_layer(o, l2_mla_w_out, y, l2_ffn_norm, l2_ffn_w_gate_up, l2_ffn_w_down)
    o = _deltanet_mixer(y, l3_mix_norm, l3_dn_w_in, l3_dn_conv_w, l3_dn_a_log, l3_dn_dt_bias, l3_dn_out_norm,
                        **shape)
    y = close_layer(o, l3_dn_w_out, y, l3_ffn_norm, l3_ffn_w_gate_up, l3_ffn_w_down)
    return y.reshape(batch, seq, d_model)
```

```python
import functools

import jax
import jax.numpy as jnp
from jax import lax
from jax.experimental import pallas as pl
from jax.experimental.pallas import tpu as pltpu

F32 = jnp.float32
BF16 = jnp.bfloat16

NORM_EPS = 1e-6
LANES = 128
HEAD_DIM = 128
N_HEADS = 8
DN_CONV = 4
DN_CHUNK = 128
MLA_ROPE = 64
MLA_QK = 192
MLA_Q_RANK = 256
MLA_KV_RANK = 128
MLA_QPAD = 256
ROPE_THETA = 10000.0
LOG2E = 1.4426950408889634
MLA_MAX_BOUND = 48.0
EXP2_UNDERFLOW = 150.0
VMEM_LIMIT = 56 * 1024 * 1024
FFN_ROWS = 512


def _params(*sem):
    return pltpu.CompilerParams(dimension_semantics=sem, vmem_limit_bytes=VMEM_LIMIT)


def _rms(x):
    return x * lax.rsqrt(jnp.mean(x * x, axis=-1, keepdims=True) + NORM_EPS)


def _sigmoid(x):
    return 1.0 / (1.0 + jnp.exp(-x))


def _silu(x):
    return x / (1.0 + jnp.exp2(x * (-LOG2E)))


def _softplus(x):
    return jnp.maximum(x, 0.0) + jnp.log(1.0 + jnp.exp(-jnp.abs(x)))


def _softplus2(x):
    return jnp.maximum(x, 0.0) + jnp.log(1.0 + jnp.exp2(-jnp.abs(x))) * LOG2E


def _dot(a, b):
    return jnp.dot(a, b, preferred_element_type=F32)


def _dot_nt(a, b):
    return lax.dot_general(a, b, (((1,), (1,)), ((), ())), preferred_element_type=F32)


def _split3(x):
    hi = x.astype(BF16)
    r = x - hi.astype(F32)
    mid = r.astype(BF16)
    lo = (r - mid.astype(F32)).astype(BF16)
    return hi, mid, lo


def _plain_epilogue(acc, j, extra, o_ref):
    del j, extra
    o_ref[...] = acc.astype(o_ref.dtype)


def _sb_epilogue(acc, j, extra, o_ref):
    qg_ref, kg_ref = extra

    def head_norm(gain, scale):
        for h in range(acc.shape[1] // HEAD_DIM):
            sl = slice(h * HEAD_DIM, (h + 1) * HEAD_DIM)
            o_ref[:, sl] = (_rms(acc[:, sl]) * gain * scale).astype(o_ref.dtype)

    @pl.when(j == 0)
    def _():
        head_norm(qg_ref[...], HEAD_DIM ** -0.5 * LOG2E)

    @pl.when(j == 1)
    def _():
        head_norm(kg_ref[...], 1.0)

    @pl.when(j == 2)
    def _():
        o_ref[...] = acc.astype(o_ref.dtype)


def _norm_matmul_body(x_ref, g_ref, w_ref, *rest, epilogue, n_extra):
    extra, o_ref, h_ref = rest[:n_extra], rest[n_extra], rest[n_extra + 1]
    j = pl.program_id(1)

    @pl.when(j == 0)
    def _():
        h_ref[...] = (_rms(x_ref[...]) * g_ref[...]).astype(BF16)

    epilogue(_dot(h_ref[...], w_ref[...]), j, extra, o_ref)


def _norm_matmul(x, gain, w, *, tm, tn, out_dtype, epilogue=_plain_epilogue, extra=()):
    m, d = x.shape
    n = w.shape[1]
    tm = min(tm, m)
    extra_specs = [pl.BlockSpec(e.shape, lambda i, j: (0, 0)) for e in extra]
    return pl.pallas_call(
        functools.partial(_norm_matmul_body, epilogue=epilogue, n_extra=len(extra)),
        grid=(m // tm, n // tn),
        in_specs=[pl.BlockSpec((tm, d), lambda i, j: (i, 0)),
                  pl.BlockSpec((1, d), lambda i, j: (0, 0)),
                  pl.BlockSpec((d, tn), lambda i, j: (0, j))] + extra_specs,
        out_specs=pl.BlockSpec((tm, tn), lambda i, j: (i, j)),
        out_shape=jax.ShapeDtypeStruct((m, n), out_dtype),
        scratch_shapes=[pltpu.VMEM((tm, d), BF16)],
        compiler_params=_params("parallel", "arbitrary"),
        name="norm_matmul",
    )(x, gain.reshape(1, d), w, *extra)


def _proj_ffn_body(a_ref, wo_ref, x_ref, g_ref, wg_ref, wu_ref, wd_ref, o_ref, h_ref):
    @pl.when(pl.program_id(1) == 0)
    def _():
        y = x_ref[...] + _dot(a_ref[...], wo_ref[...])
        h_ref[...] = (_rms(y) * g_ref[...]).astype(BF16)
        o_ref[...] = y

    h = h_ref[...]
    gate = _dot(h, wg_ref[...])
    up = _dot(h, wu_ref[...])
    act = (_silu(gate) * up).astype(BF16)
    o_ref[...] += _dot(act, wd_ref[...])


def _proj_ffn(a, w_out, x, gain, w_gate_up, w_down, *, tm, th):
    m, d = x.shape
    ka = a.shape[1]
    hidden = w_down.shape[0]
    nh = hidden // th
    tm = min(tm, m)
    once = dict(pipeline_mode=pl.Buffered(1))
    resident = once if nh == 1 else {}
    return pl.pallas_call(
        _proj_ffn_body,
        grid=(m // tm, nh),
        in_specs=[pl.BlockSpec((tm, ka), lambda i, j: (i, 0)),
                  pl.BlockSpec((ka, d), lambda i, j: (0, 0), **once),
                  pl.BlockSpec((tm, d), lambda i, j: (i, 0)),
                  pl.BlockSpec((1, d), lambda i, j: (0, 0)),
                  pl.BlockSpec((d, th), lambda i, j: (0, j), **resident),
                  pl.BlockSpec((d, th), lambda i, j: (0, nh + j), **resident),
                  pl.BlockSpec((th, d), lambda i, j: (j, 0), **resident)],
        out_specs=pl.BlockSpec((tm, d), lambda i, j: (i, 0)),
        out_shape=jax.ShapeDtypeStruct((m, d), F32),
        scratch_shapes=[pltpu.VMEM((tm, d), BF16)],
        compiler_params=_params("parallel", "arbitrary"),
        name="proj_ffn",
    )(a, w_out, x, gain.reshape(1, d), w_gate_up, w_gate_up, w_down)


def _sb_attn_body(qg_ref, kg_ref, q_ref, k_ref, v_ref, o_ref, *, tile, nsub, kunroll):
    qi = pl.program_id(2)
    qs = [q_ref[s * tile:(s + 1) * tile, :] for s in range(nsub)]
    row = lax.broadcasted_iota(jnp.int32, (tile, tile), 0)
    col = lax.broadcasted_iota(jnp.int32, (tile, tile), 1)
    suffix_ones = (row >= col).astype(BF16)
    suffix_ones2 = jnp.concatenate([suffix_ones, suffix_ones], axis=0)
    past = col < row

    def sweep(starts, jobs, stays, accs):
        ks = [k_ref[pl.ds(st, tile), :] for st in starts]
        vs = [v_ref[pl.ds(st, tile), :] for st in starts]
        zs = [_dot_nt(qs[s], ks[t]) for t, s, _ in jobs]
        sps = [_softplus2(z) for z in zs]
        sps = [jnp.where(past, sp, 0.0) if masked else sp for (_, _, masked), sp in zip(jobs, sps)]
        his = [sp.astype(BF16) for sp in sps]
        los = [(sp - hi.astype(F32)).astype(BF16) for sp, hi in zip(sps, his)]
        cums = [_dot(jnp.concatenate([hi, lo], axis=1), suffix_ones2) for hi, lo in zip(his, los)]
        stays, accs = list(stays), list(accs)
        ps = []
        for (_, s, masked), z, cum in zip(jobs, zs, cums):
            p = jnp.exp2(z - cum - stays[s])
            ps.append(jnp.where(past, p, 0.0) if masked else p)
            stays[s] = stays[s] + cum[:, 0:1]
        for (t, s, _), p in zip(jobs, ps):
            accs[s] = accs[s] + _dot(p.astype(BF16), vs[t])
        return stays, accs

    stays = [jnp.zeros((tile, 1), F32)] * nsub
    accs = [jnp.zeros((tile, HEAD_DIM), F32)] * nsub
    base = qi * nsub
    own = list(reversed(range(nsub)))
    stays, accs = sweep([pl.multiple_of((base + j) * tile, tile) for j in own],
                        [(t, s, s == j) for t, j in enumerate(own) for s in range(j, nsub)], stays, accs)
    full = [(t, s, False) for t in range(kunroll) for s in range(nsub)]

    gmax = lambda r: jnp.max(jnp.abs(r[...]), axis=-1, keepdims=True)
    reach = 1.02 * (HEAD_DIM ** 0.5 * LOG2E) * gmax(qg_ref) * gmax(kg_ref) + EXP2_UNDERFLOW

    def body(carry):
        i, st, ac = carry[0], carry[2:2 + nsub], carry[2 + nsub:]
        first = base - 1 - i * kunroll
        starts = [pl.multiple_of((first - u) * tile, tile) for u in range(kunroll)]
        st, ac = sweep(starts, full, st, ac)
        slack = functools.reduce(jnp.minimum, st) - reach
        saturated = (jnp.min(slack) >= 0.0).astype(jnp.int32)
        return (i + 1, saturated) + tuple(st) + tuple(ac)

    def unfinished(carry):
        return jnp.logical_and(carry[0] < base // kunroll, carry[1] == 0)

    carry = lax.while_loop(unfinished, body, (jnp.int32(0), jnp.int32(0)) + tuple(stays) + tuple(accs))
    for s in range(nsub):
        o_ref[s * tile:(s + 1) * tile, :] = carry[2 + nsub + s].astype(o_ref.dtype)


def _sb_attention(qkv, q_gain, k_gain, *, batch, seq, tile, nsub, kunroll):
    m = qkv.shape[0]
    nq = seq // (tile * nsub)
    assert nsub % kunroll == 0
    gain_spec = pl.BlockSpec((1, HEAD_DIM), lambda b, h, i: (0, 0))
    return pl.pallas_call(
        functools.partial(_sb_attn_body, tile=tile, nsub=nsub, kunroll=kunroll),
        grid=(batch, N_HEADS, nq),
        in_specs=[gain_spec, gain_spec,
                  pl.BlockSpec((tile * nsub, HEAD_DIM), lambda b, h, i: (b * nq + i, h)),
                  pl.BlockSpec((seq, HEAD_DIM), lambda b, h, i: (b, N_HEADS + h)),
                  pl.BlockSpec((seq, HEAD_DIM), lambda b, h, i: (b, 2 * N_HEADS + h))],
        out_specs=pl.BlockSpec((tile * nsub, HEAD_DIM), lambda b, h, i: (b * nq + i, h)),
        out_shape=jax.ShapeDtypeStruct((m, N_HEADS * HEAD_DIM), BF16),
        compiler_params=_params("parallel", "parallel", "arbitrary"),
        name="sb_attention",
    )(q_gain, k_gain, qkv, qkv, qkv)


def _rope(x, cos_t, sin_a, sin_b):
    return x * cos_t + pltpu.roll(x, 96, 1) * sin_a + pltpu.roll(x, 32, 1) * sin_b


def _rms_rope_part(x, gain):
    ms = jnp.sum(x * x, axis=-1, keepdims=True) * (1.0 / MLA_ROPE)
    return x * lax.rsqrt(ms + NORM_EPS) * gain


def _mla_score_scale():
    return MLA_QK ** -0.5 * LOG2E


def _mla_score_bound(q_nope_g, q_rope_g, k_nope_g, k_rope_g):
    gmax = lambda g: jnp.max(jnp.abs(g), axis=-1, keepdims=True)
    return 1.02 * _mla_score_scale() * (HEAD_DIM * gmax(q_nope_g) * gmax(k_nope_g)
                                         + MLA_ROPE * gmax(q_rope_g) * gmax(k_rope_g))


def _mla_prep_body(d_ref, wq_ref, wkv_ref, qa_ref, kva_ref, qn_ref, qr_ref, kn_ref, kr_ref,
                   cos_ref, sa_ref, sb_ref, q_out, kn_out, v_out, kr_out):
    d = d_ref[...]
    cos_t, sin_a, sin_b = cos_ref[...], sa_ref[...], sb_ref[...]
    scale = _mla_score_scale()
    lane = lax.broadcasted_iota(jnp.int32, (1, LANES), 1)
    shift = _mla_score_bound(qn_ref[...], qr_ref[...], kn_ref[...], kr_ref[...])
    q_bias = jnp.where(lane == MLA_ROPE, -shift, 0.0)
    k_bias = jnp.where(lane == MLA_ROPE, 1.0, 0.0)
    cq = (_rms(d[:, :MLA_Q_RANK]) * qa_ref[...]).astype(BF16)
    ckv = (_rms(d[:, MLA_Q_RANK:MLA_Q_RANK + MLA_KV_RANK]) * kva_ref[...]).astype(BF16)
    qp = _dot(cq, wq_ref[...])
    kv = _dot(ckv, wkv_ref[...])
    for h in range(N_HEADS):
        c0 = h * MLA_QPAD
        nope = _rms(qp[:, c0:c0 + HEAD_DIM]) * qn_ref[...]
        q_out[:, c0:c0 + HEAD_DIM] = (nope * scale).astype(BF16)
        rot = _rope(_rms_rope_part(qp[:, c0 + HEAD_DIM:c0 + MLA_QPAD], qr_ref[...]), cos_t, sin_a, sin_b)
        q_out[:, c0 + HEAD_DIM:c0 + MLA_QPAD] = (rot * scale + q_bias).astype(BF16)
        ks = slice(h * HEAD_DIM, (h + 1) * HEAD_DIM)
        kn_out[:, ks] = (_rms(kv[:, ks]) * kn_ref[...]).astype(BF16)
    v_out[...] = kv[:, N_HEADS * HEAD_DIM:].astype(BF16)
    k_rope = d[:, MLA_Q_RANK + MLA_KV_RANK:]
    kr_out[...] = (_rope(_rms_rope_part(k_rope, kr_ref[...]), cos_t, sin_a, sin_b) + k_bias).astype(BF16)


def _mla_prep(down, wq, wkv, gains, tables, *, seq, tm):
    m = down.shape[0]
    nt = seq // tm
    full = lambda a: pl.BlockSpec(a.shape, lambda i: (0, 0))
    tab = pl.BlockSpec((tm, LANES), lambda i: (i % nt, 0))
    row = lambda n: pl.BlockSpec((tm, n), lambda i: (i, 0))
    return pl.pallas_call(
        _mla_prep_body,
        grid=(m // tm,),
        in_specs=[row(down.shape[1]), full(wq), full(wkv)] + [full(g) for g in gains] + [tab, tab, tab],
        out_specs=[row(N_HEADS * MLA_QPAD), row(N_HEADS * HEAD_DIM), row(N_HEADS * HEAD_DIM), row(LANES)],
        out_shape=[jax.ShapeDtypeStruct((m, N_HEADS * MLA_QPAD), BF16),
                   jax.ShapeDtypeStruct((m, N_HEADS * HEAD_DIM), BF16),
                   jax.ShapeDtypeStruct((m, N_HEADS * HEAD_DIM), BF16),
                   jax.ShapeDtypeStruct((m, LANES), BF16)],
        compiler_params=_params("parallel"),
        name="mla_prep",
    )(down, wq, wkv, *gains, *tables)


def _mla_attn_body(q_ref, kn_ref, kr_ref, v_ref, o_ref, *, tile, nsub):
    qi = pl.program_id(2)
    qs = [q_ref[s * tile:(s + 1) * tile, :] for s in range(nsub)]
    row = lax.broadcasted_iota(jnp.int32, (tile, tile), 0)
    col = lax.broadcasted_iota(jnp.int32, (tile, tile), 1)
    visible = col <= row

    def sweep(start, subs, carry, diag_sub):
        k = jnp.concatenate([kn_ref[pl.ds(start, tile), :], kr_ref[pl.ds(start, tile), :]], axis=1)
        v = v_ref[pl.ds(start, tile), :]
        ss = [_dot_nt(qs[s], k) for s in subs]
        ss = [jnp.where(visible, x, -1e30) if s == diag_sub else x for s, x in zip(subs, ss)]
        m_new = [jnp.maximum(carry[s][0], jnp.max(x, axis=-1, keepdims=True)) for s, x in zip(subs, ss)]
        ps = [jnp.exp2(x - mn) for x, mn in zip(ss, m_new)]
        carry = list(carry)
        for s, p, mn in zip(subs, ps, m_new):
            m_run, l_run, acc = carry[s]
            alpha = jnp.exp2(m_run - mn)
            carry[s] = (mn, alpha * l_run + jnp.sum(p, axis=-1, keepdims=True),
                        alpha * acc + _dot(p.astype(BF16), v))
        return carry

    def flat(carry):
        return tuple(x for c in carry for x in c)

    def body(i, fc):
        carry = [tuple(fc[3 * s:3 * s + 3]) for s in range(nsub)]
        return flat(sweep(pl.multiple_of(i * tile, tile), list(range(nsub)), carry, None))

    base = qi * nsub
    init = [(jnp.full((tile, 1), -1e30, F32), jnp.zeros((tile, 1), F32),
             jnp.zeros((tile, HEAD_DIM), F32))] * nsub
    fc = lax.fori_loop(0, base, body, flat(init))
    carry = [tuple(fc[3 * s:3 * s + 3]) for s in range(nsub)]
    for j in range(nsub):
        start = pl.multiple_of((base + j) * tile, tile)
        carry = sweep(start, list(range(j, nsub)), carry, j)
    for s in range(nsub):
        o_ref[s * tile:(s + 1) * tile, :] = (carry[s][2] / carry[s][1]).astype(o_ref.dtype)


def _mla_attn_bounded_body(q_ref, kn_ref, kr_ref, v_ref, o_ref, *, tile, nsub, kunroll):
    qi = pl.program_id(2)
    qs = [q_ref[s * tile:(s + 1) * tile, :] for s in range(nsub)]
    row = lax.broadcasted_iota(jnp.int32, (tile, tile), 0)
    col = lax.broadcasted_iota(jnp.int32, (tile, tile), 1)
    visible = col <= row
    ones_col = (lax.broadcasted_iota(jnp.int32, (tile, LANES), 1) == 0).astype(BF16)

    def sweep(starts, jobs, accs):
        ks = [jnp.concatenate([kn_ref[pl.ds(st, tile), :], kr_ref[pl.ds(st, tile), :]], axis=1) for st in starts]
        vs = [jnp.concatenate([v_ref[pl.ds(st, tile), :], ones_col], axis=1) for st in starts]
        ss = [_dot_nt(qs[s], ks[t]) for t, s, _ in jobs]
        ss = [jnp.where(visible, x, -1e30) if masked else x for (_, _, masked), x in zip(jobs, ss)]
        ps = [jnp.exp2(x).astype(BF16) for x in ss]
        accs = list(accs)
        for (t, s, _), p in zip(jobs, ps):
            accs[s] = accs[s] + _dot(p, vs[t])
        return accs

    full = [(t, s, False) for t in range(kunroll) for s in range(nsub)]

    def body(i, accs):
        starts = [pl.multiple_of((i * kunroll + u) * tile, tile) for u in range(kunroll)]
        return tuple(sweep(starts, full, accs))

    base = qi * nsub
    accs = lax.fori_loop(0, base // kunroll, body, (jnp.zeros((tile, 2 * HEAD_DIM), F32),) * nsub)
    accs = sweep([pl.multiple_of((base + j) * tile, tile) for j in range(nsub)],
                 [(j, s, s == j) for j in range(nsub) for s in range(j, nsub)], accs)
    for s in range(nsub):
        o_ref[s * tile:(s + 1) * tile, :] = (
            accs[s][:, :HEAD_DIM] / accs[s][:, HEAD_DIM:HEAD_DIM + 1]).astype(o_ref.dtype)


def _mla_attention(q, kn, kr, v, *, bounded, batch, seq, tile, nsub, kunroll):
    m = q.shape[0]
    nq = seq // (tile * nsub)
    assert nsub % kunroll == 0
    body = (functools.partial(_mla_attn_bounded_body, kunroll=kunroll) if bounded else _mla_attn_body)
    return pl.pallas_call(
        functools.partial(body, tile=tile, nsub=nsub),
        grid=(batch, N_HEADS, nq),
        in_specs=[pl.BlockSpec((tile * nsub, MLA_QPAD), lambda b, h, i: (b * nq + i, h)),
                  pl.BlockSpec((seq, HEAD_DIM), lambda b, h, i: (b, h)),
                  pl.BlockSpec((seq, LANES), lambda b, h, i: (b, 0)),
                  pl.BlockSpec((seq, HEAD_DIM), lambda b, h, i: (b, h))],
        out_specs=pl.BlockSpec((tile * nsub, HEAD_DIM), lambda b, h, i: (b * nq + i, h)),
        out_shape=jax.ShapeDtypeStruct((m, N_HEADS * HEAD_DIM), BF16),
        compiler_params=_params("parallel", "parallel", "arbitrary"),
        name="mla_attention_bounded" if bounded else "mla_attention_online",
    )(q, kn, kr, v)


def _dn_body(x_ref, ab_ref, cw_ref, alog_ref, dtb_ref, onorm_ref, o_ref, state_ref, ext_ref):
    c = DN_CHUNK
    width = N_HEADS * HEAD_DIM

    @pl.when(pl.program_id(1) == 0)
    def _():
        state_ref[...] = jnp.zeros_like(state_ref)
        ext_ref[0:8, :] = jnp.zeros((8, 3 * width), F32)

    ext_ref[8:8 + c, :] = x_ref[:, 0:3 * width].astype(F32)

    def conv_silu(col):
        sl = slice(col, col + HEAD_DIM)
        y = cw_ref[DN_CONV - 1:DN_CONV, sl] * ext_ref[8:8 + c, sl]
        for j in range(DN_CONV - 1):
            y = y + cw_ref[j:j + 1, sl] * ext_ref[5 + j:5 + j + c, sl]
        return _silu(y)

    ab = ab_ref[...]
    g_all = -jnp.exp(alog_ref[...]) * _softplus(ab + dtb_ref[...])
    beta_all = _sigmoid(ab)
    row = lax.broadcasted_iota(jnp.int32, (c, c), 0)
    col = lax.broadcasted_iota(jnp.int32, (c, c), 1)
    causal = row >= col
    strict = row > col
    eye = (row == col).astype(F32)
    g_parts = _split3(g_all)
    prefix_ones = causal.astype(BF16)
    all_ones = jnp.ones((c, c), BF16)
    gc_all = sum(_dot(prefix_ones, p) for p in g_parts)
    gt_all = sum(_dot(all_ones, p) for p in g_parts)
    gc_rows = gc_all.T
    levels = []
    b = 1
    while b < c:
        shift = b.bit_length() - 1
        levels.append(((row >> (shift + 1)) == (col >> (shift + 1)))
                      & (((row >> shift) & 1) == 1) & (((col >> shift) & 1) == 0))
        b *= 2

    heads = range(N_HEADS)
    q, k, v = [], [], []
    for h in heads:
        qh = conv_silu(h * HEAD_DIM)
        kh = conv_silu(width + h * HEAD_DIM)
        q.append(qh * lax.rsqrt(jnp.sum(qh * qh, axis=-1, keepdims=True) + NORM_EPS) * (HEAD_DIM ** -0.5))
        k.append(kh * lax.rsqrt(jnp.sum(kh * kh, axis=-1, keepdims=True) + NORM_EPS))
        v.append(conv_silu(2 * width + h * HEAD_DIM))
    g_col = [gc_all[:, h:h + 1] for h in heads]
    g_tot = [gt_all[:, h:h + 1] for h in heads]
    beta = [beta_all[:, N_HEADS + h:N_HEADS + h + 1] for h in heads]
    decay = [jnp.exp(jnp.where(causal, g_col[h] - gc_rows[h:h + 1, :], 0.0)) for h in heads]
    k_bf = [k[h].astype(BF16) for h in heads]
    k_beta = [k[h] * beta[h] for h in heads]
    kk = [_dot_nt(k_beta[h].astype(BF16), k_bf[h]) for h in heads]
    qk = [_dot_nt(q[h].astype(BF16), k_bf[h]) for h in heads]
    lower = [jnp.where(strict, kk[h] * decay[h], 0.0) for h in heads]
    inv = [eye - jnp.where(levels[0], lower[h], 0.0) for h in heads]
    lower_bf = [lower[h].astype(BF16) for h in heads]
    for mask in levels[1:]:
        mask_bf = mask.astype(BF16)
        inv_bf = [inv[h].astype(BF16) for h in heads]
        step = [_dot(inv_bf[h], lower_bf[h] * mask_bf) for h in heads]
        inv = [inv[h] - _dot(step[h].astype(BF16), inv_bf[h]) for h in heads]
    exp_g = [jnp.exp(g_col[h]) for h in heads]
    uw = [_dot(inv[h].astype(BF16),
               jnp.concatenate([v[h] * beta[h], k_beta[h] * exp_g[h]], axis=1).astype(BF16)) for h in heads]
    state = [state_ref[h] for h in heads]
    ws_qs = [_dot(jnp.concatenate([uw[h][:, HEAD_DIM:], q[h] * exp_g[h]], axis=0).astype(BF16),
                  state[h].astype(BF16)) for h in heads]
    v_new = [(uw[h][:, :HEAD_DIM] - ws_qs[h][:c]).astype(BF16) for h in heads]
    attn = [jnp.where(causal, qk[h] * decay[h], 0.0).astype(BF16) for h in heads]
    out = [ws_qs[h][c:] + _dot(attn[h], v_new[h]) for h in heads]
    for h in heads:
        k_dec = k[h] * jnp.exp(g_tot[h] - g_col[h])
        state_ref[h] = state[h] * jnp.exp(g_tot[h]) + _dot(k_dec.T.astype(BF16), v_new[h])
    for h in heads:
        z = x_ref[:, 3 * width + h * HEAD_DIM:3 * width + (h + 1) * HEAD_DIM].astype(F32)
        o_ref[:, h * HEAD_DIM:(h + 1) * HEAD_DIM] = (
            _rms(out[h]) * onorm_ref[...] * _silu(z)).astype(o_ref.dtype)

    ext_ref[0:8, :] = ext_ref[c:c + 8, :]


def _dn_core(qkvz, ab, conv_w, a_log, dt_bias, out_norm, *, batch, seq):
    m = qkvz.shape[0]
    nc = seq // DN_CHUNK
    width = N_HEADS * HEAD_DIM
    full = lambda a: pl.BlockSpec(a.shape, lambda b, t: (0, 0))
    return pl.pallas_call(
        _dn_body,
        grid=(batch, nc),
        in_specs=[pl.BlockSpec((DN_CHUNK, 4 * width), lambda b, t: (b * nc + t, 0)),
                  pl.BlockSpec((DN_CHUNK, LANES), lambda b, t: (b * nc + t, 0)),
                  full(conv_w), full(a_log), full(dt_bias), full(out_norm)],
        out_specs=pl.BlockSpec((DN_CHUNK, width), lambda b, t: (b * nc + t, 0)),
        out_shape=jax.ShapeDtypeStruct((m, width), BF16),
        scratch_shapes=[pltpu.VMEM((N_HEADS, HEAD_DIM, HEAD_DIM), F32),
                        pltpu.VMEM((DN_CHUNK + 8, 3 * width), F32)],
        compiler_params=_params("parallel", "arbitrary"),
        name="dn_core",
    )(qkvz, ab, conv_w, a_log, dt_bias, out_norm)


def _pad_lanes(vec, n=LANES):
    return jnp.pad(vec.astype(F32), (0, n - vec.shape[0])).reshape(1, n)


def _deltanet_mixer(x, mix_norm, w_in, conv_w, a_log, dt_bias, out_norm, *, batch, seq):
    width = N_HEADS * HEAD_DIM
    w_main = w_in[:, :4 * width].astype(BF16)
    w_gate = jnp.pad(w_in[:, 4 * width:], ((0, 0), (0, LANES - 2 * N_HEADS))).astype(BF16)
    qkvz = _norm_matmul(x, mix_norm, w_main, tm=1024, tn=2048, out_dtype=BF16)
    ab = _norm_matmul(x, mix_norm, w_gate, tm=1024, tn=LANES, out_dtype=F32)
    o = _dn_core(qkvz, ab, conv_w, _pad_lanes(a_log), _pad_lanes(dt_bias), out_norm.reshape(1, HEAD_DIM),
                 batch=batch, seq=seq)
    return o


def _stick_breaking_mixer(x, mix_norm, w_qkv, q_norm, k_norm, *, batch, seq):
    gains = (q_norm.reshape(1, HEAD_DIM), k_norm.reshape(1, HEAD_DIM))
    qkv = _norm_matmul(x, mix_norm, w_qkv.astype(BF16), tm=1024, tn=N_HEADS * HEAD_DIM, out_dtype=BF16,
                       epilogue=_sb_epilogue, extra=gains)
    o = _sb_attention(qkv, *gains, batch=batch, seq=seq, tile=256, nsub=4, kunroll=1)
    return o


def _rope_tables(seq):
    half = MLA_ROPE // 2
    inv_freq = ROPE_THETA ** (-jnp.arange(0, MLA_ROPE, 2, dtype=F32) / MLA_ROPE)
    ang = jnp.arange(seq, dtype=F32)[:, None] * inv_freq[None, :]
    cos, sin, zero = jnp.cos(ang), jnp.sin(ang), jnp.zeros((seq, half), F32)
    pad = jnp.zeros((seq, LANES - MLA_ROPE), F32)
    return (jnp.concatenate([cos, cos, pad], axis=1),
            jnp.concatenate([-sin, zero, pad], axis=1),
            jnp.concatenate([zero, sin, pad], axis=1))


def _mla_mixer(x, mix_norm, w_down, q_a_norm, kv_a_norm, w_uq, w_ukv, q_nope_norm, q_rope_norm,
               k_nope_norm, k_rope_norm, *, batch, seq):
    down_w = jnp.pad(w_down, ((0, 0), (0, 512 - w_down.shape[1]))).astype(BF16)
    wq = jnp.pad(w_uq.reshape(MLA_Q_RANK, N_HEADS, MLA_QK),
                 ((0, 0), (0, 0), (0, MLA_QPAD - MLA_QK))).reshape(MLA_Q_RANK, N_HEADS * MLA_QPAD).astype(BF16)
    wkv = w_ukv.reshape(MLA_KV_RANK, N_HEADS, 2, HEAD_DIM).transpose(0, 2, 1, 3).reshape(
        MLA_KV_RANK, 2 * N_HEADS * HEAD_DIM).astype(BF16)
    down = _norm_matmul(x, mix_norm, down_w, tm=1024, tn=512, out_dtype=F32)
    gains = (q_a_norm.reshape(1, -1), kv_a_norm.reshape(1, -1), q_nope_norm.reshape(1, -1),
             _pad_lanes(q_rope_norm), k_nope_norm.reshape(1, -1), _pad_lanes(k_rope_norm))
    q, kn, v, kr = _mla_prep(down, wq, wkv, gains, _rope_tables(seq), seq=seq, tm=512)
    attend = functools.partial(_mla_attention, q, kn, kr, v, batch=batch, seq=seq, tile=256, nsub=4,
                               kunroll=4)
    bound = _mla_score_bound(gains[2], gains[3], gains[4], gains[5])[0, 0]
    o = lax.cond(bound < MLA_MAX_BOUND, lambda: attend(bounded=True), lambda: attend(bounded=False))
    return o


def kernel(x, l0_mix_norm, l0_dn_w_in, l0_dn_conv_w, l0_dn_a_log, l0_dn_dt_bias, l0_dn_out_norm, l0_dn_w_out, l0_ffn_norm, l0_ffn_w_gate_up, l0_ffn_w_down, l1_mix_norm, l1_sb_w_qkv, l1_sb_q_norm, l1_sb_k_norm, l1_sb_w_out, l1_ffn_norm, l1_ffn_w_gate_up, l1_ffn_w_down, l2_mix_norm, l2_mla_w_down, l2_mla_q_a_norm, l2_mla_kv_a_norm, l2_mla_w_uq, l2_mla_w_ukv, l2_mla_q_nope_norm, l2_mla_q_rope_norm, l2_mla_k_nope_norm, l2_mla_k_rope_norm, l2_mla_w_out, l2_ffn_norm, l2_ffn_w_gate_up, l2_ffn_w_down, l3_mix_norm, l3_dn_w_in, l3_dn_conv_w, l3_dn_a_log, l3_dn_dt_bias, l3_dn_out_norm, l3_dn_w_out, l3_ffn_norm, l3_ffn_w_gate_up, l3_ffn_w_down):
    batch, seq, d_model = x.shape
    shape = dict(batch=batch, seq=seq)
    def close_layer(mixed, w_out, y, g, wgu, wd):
        return _proj_ffn(mixed, w_out.astype(BF16), y, g, wgu.astype(BF16), wd.astype(BF16),
                         tm=FFN_ROWS, th=wd.shape[0])

    y = x.reshape(batch * seq, d_model)
    o = _deltanet_mixer(y, l0_mix_norm, l0_dn_w_in, l0_dn_conv_w, l0_dn_a_log, l0_dn_dt_bias, l0_dn_out_norm,
                        **shape)
    y = close_layer(o, l0_dn_w_out, y, l0_ffn_norm, l0_ffn_w_gate_up, l0_ffn_w_down)
    o = _stick_breaking_mixer(y, l1_mix_norm, l1_sb_w_qkv, l1_sb_q_norm, l1_sb_k_norm, **shape)
    y = close_layer(o, l1_sb_w_out, y, l1_ffn_norm, l1_ffn_w_gate_up, l1_ffn_w_down)
    o = _mla_mixer(y, l2_mix_norm, l2_mla_w_down, l2_mla_q_a_norm, l2_mla_kv_a_norm, l2_mla_w_uq, l2_mla_w_ukv,
                   l2_mla_q_nope_norm, l2_mla_q_rope_norm, l2_mla_k_nope_norm, l2_mla_k_rope_norm, **shape)
    y = close_layer(o, l2_mla_w_out, y, l2_ffn_norm, l2_ffn_w_gate_up, l2_ffn_w_down)
    o = _deltanet_mixer(y, l3_mix_norm, l3_dn_w_in, l3_dn_conv_w, l3_dn_a_log, l3_dn_dt_bias, l3_dn_out_norm,
                        **shape)
    y = close_layer(o, l3_dn_w_out, y, l3_ffn_norm, l3_ffn_w_gate_up, l3_ffn_w_down)
    return y.reshape(batch, seq, d_model)
```

```python
import functools

import jax
import jax.numpy as jnp
from jax import lax
from jax.experimental import pallas as pl
from jax.experimental.pallas import tpu as pltpu

F32 = jnp.float32
BF16 = jnp.bfloat16

NORM_EPS = 1e-6
LANES = 128
HEAD_DIM = 128
N_HEADS = 8
DN_CONV = 4
DN_CHUNK = 128
MLA_ROPE = 64
MLA_QK = 192
MLA_Q_RANK = 256
MLA_KV_RANK = 128
MLA_QPAD = 256
ROPE_THETA = 10000.0
LOG2E = 1.4426950408889634
MLA_MAX_BOUND = 48.0
EXP2_UNDERFLOW = 150.0
VMEM_LIMIT = 56 * 1024 * 1024
FFN_ROWS = 512


def _params(*sem):
    return pltpu.CompilerParams(dimension_semantics=sem, vmem_limit_bytes=VMEM_LIMIT)


def _rms(x):
    return x * lax.rsqrt(jnp.mean(x * x, axis=-1, keepdims=True) + NORM_EPS)


def _sigmoid(x):
    return 1.0 / (1.0 + jnp.exp(-x))


def _silu(x):
    return x / (1.0 + jnp.exp2(x * (-LOG2E)))


def _softplus(x):
    return jnp.maximum(x, 0.0) + jnp.log(1.0 + jnp.exp(-jnp.abs(x)))


def _softplus2(x):
    return jnp.maximum(x, 0.0) + jnp.log(1.0 + jnp.exp2(-jnp.abs(x))) * LOG2E


def _dot(a, b):
    return jnp.dot(a, b, preferred_element_type=F32)


def _dot_nt(a, b):
    return lax.dot_general(a, b, (((1,), (1,)), ((), ())), preferred_element_type=F32)


def _split3(x):
    hi = x.astype(BF16)
    r = x - hi.astype(F32)
    mid = r.astype(BF16)
    lo = (r - mid.astype(F32)).astype(BF16)
    return hi, mid, lo


def _plain_epilogue(acc, j, extra, o_ref):
    del j, extra
    o_ref[...] = acc.astype(o_ref.dtype)


def _sb_epilogue(acc, j, extra, o_ref):
    qg_ref, kg_ref = extra

    def head_norm(gain, scale):
        for h in range(acc.shape[1] // HEAD_DIM):
            sl = slice(h * HEAD_DIM, (h + 1) * HEAD_DIM)
            o_ref[:, sl] = (_rms(acc[:, sl]) * gain * scale).astype(o_ref.dtype)

    @pl.when(j == 0)
    def _():
        head_norm(qg_ref[...], HEAD_DIM ** -0.5 * LOG2E)

    @pl.when(j == 1)
    def _():
        head_norm(kg_ref[...], 1.0)

    @pl.when(j == 2)
    def _():
        o_ref[...] = acc.astype(o_ref.dtype)


def _norm_matmul_body(x_ref, g_ref, w_ref, *rest, epilogue, n_extra):
    extra, o_ref, h_ref = rest[:n_extra], rest[n_extra], rest[n_extra + 1]
    j = pl.program_id(1)

    @pl.when(j == 0)
    def _():
        h_ref[...] = (_rms(x_ref[...]) * g_ref[...]).astype(BF16)

    epilogue(_dot(h_ref[...], w_ref[...]), j, extra, o_ref)


def _norm_matmul(x, gain, w, *, tm, tn, out_dtype, epilogue=_plain_epilogue, extra=()):
    m, d = x.shape
    n = w.shape[1]
    tm = min(tm, m)
    extra_specs = [pl.BlockSpec(e.shape, lambda i, j: (0, 0)) for e in extra]
    return pl.pallas_call(
        functools.partial(_norm_matmul_body, epilogue=epilogue, n_extra=len(extra)),
        grid=(m // tm, n // tn),
        in_specs=[pl.BlockSpec((tm, d), lambda i, j: (i, 0)),
                  pl.BlockSpec((1, d), lambda i, j: (0, 0)),
                  pl.BlockSpec((d, tn), lambda i, j: (0, j))] + extra_specs,
        out_specs=pl.BlockSpec((tm, tn), lambda i, j: (i, j)),
        out_shape=jax.ShapeDtypeStruct((m, n), out_dtype),
        scratch_shapes=[pltpu.VMEM((tm, d), BF16)],
        compiler_params=_params("parallel", "arbitrary"),
        name="norm_matmul",
    )(x, gain.reshape(1, d), w, *extra)


def _dn_proj_body(x_ref, g_ref, w_ref, wgate_ref, o_ref, ab_ref, h_ref):
    @pl.when(pl.program_id(1) == 0)
    def _():
        h = (_rms(x_ref[...]) * g_ref[...]).astype(BF16)
        h_ref[...] = h
        ab_ref[...] = _dot(h, wgate_ref[...])

    o_ref[...] = _dot(h_ref[...], w_ref[...]).astype(o_ref.dtype)


def _dn_proj(x, gain, w, w_gate, *, tm, tn):
    m, d = x.shape
    n, ng = w.shape[1], w_gate.shape[1]
    tm = min(tm, m)
    return pl.pallas_call(
        _dn_proj_body,
        grid=(m // tm, n // tn),
        in_specs=[pl.BlockSpec((tm, d), lambda i, j: (i, 0)),
                  pl.BlockSpec((1, d), lambda i, j: (0, 0)),
                  pl.BlockSpec((d, tn), lambda i, j: (0, j)),
                  pl.BlockSpec((d, ng), lambda i, j: (0, 0))],
        out_specs=[pl.BlockSpec((tm, tn), lambda i, j: (i, j)),
                   pl.BlockSpec((tm, ng), lambda i, j: (i, 0))],
        out_shape=[jax.ShapeDtypeStruct((m, n), BF16), jax.ShapeDtypeStruct((m, ng), F32)],
        scratch_shapes=[pltpu.VMEM((tm, d), BF16)],
        compiler_params=_params("parallel", "arbitrary"),
        name="dn_proj",
    )(x, gain.reshape(1, d), w, w_gate)


def _proj_ffn_body(a_ref, wo_ref, x_ref, g_ref, wg_ref, wu_ref, wd_ref, o_ref, h_ref):
    @pl.when(pl.program_id(1) == 0)
    def _():
        y = x_ref[...] + _dot(a_ref[...], wo_ref[...])
        h_ref[...] = (_rms(y) * g_ref[...]).astype(BF16)
        o_ref[...] = y

    h = h_ref[...]
    gate = _dot(h, wg_ref[...])
    up = _dot(h, wu_ref[...])
    act = (_silu(gate) * up).astype(BF16)
    o_ref[...] += _dot(act, wd_ref[...])


def _proj_ffn(a, w_out, x, gain, w_gate_up, w_down, *, tm, th):
    m, d = x.shape
    ka = a.shape[1]
    hidden = w_down.shape[0]
    nh = hidden // th
    tm = min(tm, m)
    once = dict(pipeline_mode=pl.Buffered(1))
    resident = once if nh == 1 else {}
    return pl.pallas_call(
        _proj_ffn_body,
        grid=(m // tm, nh),
        in_specs=[pl.BlockSpec((tm, ka), lambda i, j: (i, 0)),
                  pl.BlockSpec((ka, d), lambda i, j: (0, 0), **once),
                  pl.BlockSpec((tm, d), lambda i, j: (i, 0)),
                  pl.BlockSpec((1, d), lambda i, j: (0, 0)),
                  pl.BlockSpec((d, th), lambda i, j: (0, j), **resident),
                  pl.BlockSpec((d, th), lambda i, j: (0, nh + j), **resident),
                  pl.BlockSpec((th, d), lambda i, j: (j, 0), **resident)],
        out_specs=pl.BlockSpec((tm, d), lambda i, j: (i, 0)),
        out_shape=jax.ShapeDtypeStruct((m, d), F32),
        scratch_shapes=[pltpu.VMEM((tm, d), BF16)],
        compiler_params=_params("parallel", "arbitrary"),
        name="proj_ffn",
    )(a, w_out, x, gain.reshape(1, d), w_gate_up, w_gate_up, w_down)


def _sb_attn_body(qg_ref, kg_ref, q_ref, k_ref, v_ref, o_ref, *, tile, nsub, kunroll):
    qi = pl.program_id(2)
    qs = [q_ref[s * tile:(s + 1) * tile, :] for s in range(nsub)]
    row = lax.broadcasted_iota(jnp.int32, (tile, tile), 0)
    col = lax.broadcasted_iota(jnp.int32, (tile, tile), 1)
    suffix_ones = (row >= col).astype(BF16)
    suffix_ones2 = jnp.concatenate([suffix_ones, suffix_ones], axis=0)
    past = col < row

    def sweep(starts, jobs, stays, accs):
        ks = [k_ref[pl.ds(st, tile), :] for st in starts]
        vs = [v_ref[pl.ds(st, tile), :] for st in starts]
        zs = [_dot_nt(qs[s], ks[t]) for t, s, _ in jobs]
        sps = [_softplus2(z) for z in zs]
        sps = [jnp.where(past, sp, 0.0) if masked else sp for (_, _, masked), sp in zip(jobs, sps)]
        his = [sp.astype(BF16) for sp in sps]
        los = [(sp - hi.astype(F32)).astype(BF16) for sp, hi in zip(sps, his)]
        cums = [_dot(jnp.concatenate([hi, lo], axis=1), suffix_ones2) for hi, lo in zip(his, los)]
        stays, accs = list(stays), list(accs)
        ps = []
        for (_, s, masked), z, cum in zip(jobs, zs, cums):
            p = jnp.exp2(z - cum - stays[s])
            ps.append(jnp.where(past, p, 0.0) if masked else p)
            stays[s] = stays[s] + cum[:, 0:1]
        for (t, s, _), p in zip(jobs, ps):
            accs[s] = accs[s] + _dot(p.astype(BF16), vs[t])
        return stays, accs

    stays = [jnp.zeros((tile, 1), F32)] * nsub
    accs = [jnp.zeros((tile, HEAD_DIM), F32)] * nsub
    base = qi * nsub
    own = list(reversed(range(nsub)))
    stays, accs = sweep([pl.multiple_of((base + j) * tile, tile) for j in own],
                        [(t, s, s == j) for t, j in enumerate(own) for s in range(j, nsub)], stays, accs)
    full = [(t, s, False) for t in range(kunroll) for s in range(nsub)]

    gmax = lambda r: jnp.max(jnp.abs(r[...]), axis=-1, keepdims=True)
    reach = 1.02 * (HEAD_DIM ** 0.5 * LOG2E) * gmax(qg_ref) * gmax(kg_ref) + EXP2_UNDERFLOW

    def body(carry):
        i, st, ac = carry[0], carry[2:2 + nsub], carry[2 + nsub:]
        first = base - 1 - i * kunroll
        starts = [pl.multiple_of((first - u) * tile, tile) for u in range(kunroll)]
        st, ac = sweep(starts, full, st, ac)
        slack = functools.reduce(jnp.minimum, st) - reach
        saturated = (jnp.min(slack) >= 0.0).astype(jnp.int32)
        return (i + 1, saturated) + tuple(st) + tuple(ac)

    def unfinished(carry):
        return jnp.logical_and(carry[0] < base // kunroll, carry[1] == 0)

    carry = lax.while_loop(unfinished, body, (jnp.int32(0), jnp.int32(0)) + tuple(stays) + tuple(accs))
    for s in range(nsub):
        o_ref[s * tile:(s + 1) * tile, :] = carry[2 + nsub + s].astype(o_ref.dtype)


def _sb_attention(qkv, q_gain, k_gain, *, batch, seq, tile, nsub, kunroll):
    m = qkv.shape[0]
    nq = seq // (tile * nsub)
    assert nsub % kunroll == 0
    gain_spec = pl.BlockSpec((1, HEAD_DIM), lambda b, h, i: (0, 0))
    return pl.pallas_call(
        functools.partial(_sb_attn_body, tile=tile, nsub=nsub, kunroll=kunroll),
        grid=(batch, N_HEADS, nq),
        in_specs=[gain_spec, gain_spec,
                  pl.BlockSpec((tile * nsub, HEAD_DIM), lambda b, h, i: (b * nq + i, h)),
                  pl.BlockSpec((seq, HEAD_DIM), lambda b, h, i: (b, N_HEADS + h)),
                  pl.BlockSpec((seq, HEAD_DIM), lambda b, h, i: (b, 2 * N_HEADS + h))],
        out_specs=pl.BlockSpec((tile * nsub, HEAD_DIM), lambda b, h, i: (b * nq + i, h)),
        out_shape=jax.ShapeDtypeStruct((m, N_HEADS * HEAD_DIM), BF16),
        compiler_params=_params("parallel", "parallel", "arbitrary"),
        name="sb_attention",
    )(q_gain, k_gain, qkv, qkv, qkv)


def _rope(x, cos_t, sin_a, sin_b):
    return x * cos_t + pltpu.roll(x, 96, 1) * sin_a + pltpu.roll(x, 32, 1) * sin_b


def _rms_rope_part(x, gain):
    ms = jnp.sum(x * x, axis=-1, keepdims=True) * (1.0 / MLA_ROPE)
    return x * lax.rsqrt(ms + NORM_EPS) * gain


def _mla_score_scale():
    return MLA_QK ** -0.5 * LOG2E


def _mla_score_bound(q_nope_g, q_rope_g, k_nope_g, k_rope_g):
    gmax = lambda g: jnp.max(jnp.abs(g), axis=-1, keepdims=True)
    return 1.02 * _mla_score_scale() * (HEAD_DIM * gmax(q_nope_g) * gmax(k_nope_g)
                                         + MLA_ROPE * gmax(q_rope_g) * gmax(k_rope_g))


def _mla_prep_body(d_ref, wq_ref, wkv_ref, qa_ref, kva_ref, qn_ref, qr_ref, kn_ref, kr_ref,
                   cos_ref, sa_ref, sb_ref, q_out, kn_out, v_out, kr_out):
    d = d_ref[...]
    cos_t, sin_a, sin_b = cos_ref[...], sa_ref[...], sb_ref[...]
    scale = _mla_score_scale()
    lane = lax.broadcasted_iota(jnp.int32, (1, LANES), 1)
    shift = _mla_score_bound(qn_ref[...], qr_ref[...], kn_ref[...], kr_ref[...])
    q_bias = jnp.where(lane == MLA_ROPE, -shift, 0.0)
    k_bias = jnp.where(lane == MLA_ROPE, 1.0, 0.0)
    cq = (_rms(d[:, :MLA_Q_RANK]) * qa_ref[...]).astype(BF16)
    ckv = (_rms(d[:, MLA_Q_RANK:MLA_Q_RANK + MLA_KV_RANK]) * kva_ref[...]).astype(BF16)
    qp = _dot(cq, wq_ref[...])
    kv = _dot(ckv, wkv_ref[...])
    for h in range(N_HEADS):
        c0 = h * MLA_QPAD
        nope = _rms(qp[:, c0:c0 + HEAD_DIM]) * qn_ref[...]
        q_out[:, c0:c0 + HEAD_DIM] = (nope * scale).astype(BF16)
        rot = _rope(_rms_rope_part(qp[:, c0 + HEAD_DIM:c0 + MLA_QPAD], qr_ref[...]), cos_t, sin_a, sin_b)
        q_out[:, c0 + HEAD_DIM:c0 + MLA_QPAD] = (rot * scale + q_bias).astype(BF16)
        ks = slice(h * HEAD_DIM, (h + 1) * HEAD_DIM)
        kn_out[:, ks] = (_rms(kv[:, ks]) * kn_ref[...]).astype(BF16)
    v_out[...] = kv[:, N_HEADS * HEAD_DIM:].astype(BF16)
    k_rope = d[:, MLA_Q_RANK + MLA_KV_RANK:]
    kr_out[...] = (_rope(_rms_rope_part(k_rope, kr_ref[...]), cos_t, sin_a, sin_b) + k_bias).astype(BF16)


def _mla_prep(down, wq, wkv, gains, tables, *, seq, tm):
    m = down.shape[0]
    nt = seq // tm
    full = lambda a: pl.BlockSpec(a.shape, lambda i: (0, 0))
    tab = pl.BlockSpec((tm, LANES), lambda i: (i % nt, 0))
    row = lambda n: pl.BlockSpec((tm, n), lambda i: (i, 0))
    return pl.pallas_call(
        _mla_prep_body,
        grid=(m // tm,),
        in_specs=[row(down.shape[1]), full(wq), full(wkv)] + [full(g) for g in gains] + [tab, tab, tab],
        out_specs=[row(N_HEADS * MLA_QPAD), row(N_HEADS * HEAD_DIM), row(N_HEADS * HEAD_DIM), row(LANES)],
        out_shape=[jax.ShapeDtypeStruct((m, N_HEADS * MLA_QPAD), BF16),
                   jax.ShapeDtypeStruct((m, N_HEADS * HEAD_DIM), BF16),
                   jax.ShapeDtypeStruct((m, N_HEADS * HEAD_DIM), BF16),
                   jax.ShapeDtypeStruct((m, LANES), BF16)],
        compiler_params=_params("parallel"),
        name="mla_prep",
    )(down, wq, wkv, *gains, *tables)


def _mla_attn_body(q_ref, kn_ref, kr_ref, v_ref, o_ref, *, tile, nsub):
    qi = pl.program_id(2)
    qs = [q_ref[s * tile:(s + 1) * tile, :] for s in range(nsub)]
    row = lax.broadcasted_iota(jnp.int32, (tile, tile), 0)
    col = lax.broadcasted_iota(jnp.int32, (tile, tile), 1)
    visible = col <= row

    def sweep(start, subs, carry, diag_sub):
        k = jnp.concatenate([kn_ref[pl.ds(start, tile), :], kr_ref[pl.ds(start, tile), :]], axis=1)
        v = v_ref[pl.ds(start, tile), :]
        ss = [_dot_nt(qs[s], k) for s in subs]
        ss = [jnp.where(visible, x, -1e30) if s == diag_sub else x for s, x in zip(subs, ss)]
        m_new = [jnp.maximum(carry[s][0], jnp.max(x, axis=-1, keepdims=True)) for s, x in zip(subs, ss)]
        ps = [jnp.exp2(x - mn) for x, mn in zip(ss, m_new)]
        carry = list(carry)
        for s, p, mn in zip(subs, ps, m_new):
            m_run, l_run, acc = carry[s]
            alpha = jnp.exp2(m_run - mn)
            carry[s] = (mn, alpha * l_run + jnp.sum(p, axis=-1, keepdims=True),
                        alpha * acc + _dot(p.astype(BF16), v))
        return carry

    def flat(carry):
        return tuple(x for c in carry for x in c)

    def body(i, fc):
        carry = [tuple(fc[3 * s:3 * s + 3]) for s in range(nsub)]
        return flat(sweep(pl.multiple_of(i * tile, tile), list(range(nsub)), carry, None))

    base = qi * nsub
    init = [(jnp.full((tile, 1), -1e30, F32), jnp.zeros((tile, 1), F32),
             jnp.zeros((tile, HEAD_DIM), F32))] * nsub
    fc = lax.fori_loop(0, base, body, flat(init))
    carry = [tuple(fc[3 * s:3 * s + 3]) for s in range(nsub)]
    for j in range(nsub):
        start = pl.multiple_of((base + j) * tile, tile)
        carry = sweep(start, list(range(j, nsub)), carry, j)
    for s in range(nsub):
        o_ref[s * tile:(s + 1) * tile, :] = (carry[s][2] / carry[s][1]).astype(o_ref.dtype)


def _mla_attn_bounded_body(q_ref, kn_ref, kr_ref, v_ref, o_ref, *, tile, nsub, kunroll):
    qi = pl.program_id(2)
    qs = [q_ref[s * tile:(s + 1) * tile, :] for s in range(nsub)]
    row = lax.broadcasted_iota(jnp.int32, (tile, tile), 0)
    col = lax.broadcasted_iota(jnp.int32, (tile, tile), 1)
    visible = col <= row
    ones_col = (lax.broadcasted_iota(jnp.int32, (tile, LANES), 1) == 0).astype(BF16)

    def sweep(starts, jobs, accs):
        ks = [jnp.concatenate([kn_ref[pl.ds(st, tile), :], kr_ref[pl.ds(st, tile), :]], axis=1) for st in starts]
        vs = [jnp.concatenate([v_ref[pl.ds(st, tile), :], ones_col], axis=1) for st in starts]
        ss = [_dot_nt(qs[s], ks[t]) for t, s, _ in jobs]
        ss = [jnp.where(visible, x, -1e30) if masked else x for (_, _, masked), x in zip(jobs, ss)]
        ps = [jnp.exp2(x).astype(BF16) for x in ss]
        accs = list(accs)
        for (t, s, _), p in zip(jobs, ps):
            accs[s] = accs[s] + _dot(p, vs[t])
        return accs

    full = [(t, s, False) for t in range(kunroll) for s in range(nsub)]

    def body(i, accs):
        starts = [pl.multiple_of((i * kunroll + u) * tile, tile) for u in range(kunroll)]
        return tuple(sweep(starts, full, accs))

    base = qi * nsub
    accs = lax.fori_loop(0, base // kunroll, body, (jnp.zeros((tile, 2 * HEAD_DIM), F32),) * nsub)
    accs = sweep([pl.multiple_of((base + j) * tile, tile) for j in range(nsub)],
                 [(j, s, s == j) for j in range(nsub) for s in range(j, nsub)], accs)
    for s in range(nsub):
        o_ref[s * tile:(s + 1) * tile, :] = (
            accs[s][:, :HEAD_DIM] / accs[s][:, HEAD_DIM:HEAD_DIM + 1]).astype(o_ref.dtype)


def _mla_attention(q, kn, kr, v, *, bounded, batch, seq, tile, nsub, kunroll):
    m = q.shape[0]
    nq = seq // (tile * nsub)
    assert nsub % kunroll == 0
    body = (functools.partial(_mla_attn_bounded_body, kunroll=kunroll) if bounded else _mla_attn_body)
    return pl.pallas_call(
        functools.partial(body, tile=tile, nsub=nsub),
        grid=(batch, N_HEADS, nq),
        in_specs=[pl.BlockSpec((tile * nsub, MLA_QPAD), lambda b, h, i: (b * nq + i, h)),
                  pl.BlockSpec((seq, HEAD_DIM), lambda b, h, i: (b, h)),
                  pl.BlockSpec((seq, LANES), lambda b, h, i: (b, 0)),
                  pl.BlockSpec((seq, HEAD_DIM), lambda b, h, i: (b, h))],
        out_specs=pl.BlockSpec((tile * nsub, HEAD_DIM), lambda b, h, i: (b * nq + i, h)),
        out_shape=jax.ShapeDtypeStruct((m, N_HEADS * HEAD_DIM), BF16),
        compiler_params=_params("parallel", "parallel", "arbitrary"),
        name="mla_attention_bounded" if bounded else "mla_attention_online",
    )(q, kn, kr, v)


def _dn_body(x_ref, ab_ref, cw_ref, alog_ref, dtb_ref, onorm_ref, o_ref, state_ref, ext_ref):
    c = DN_CHUNK
    width = N_HEADS * HEAD_DIM

    @pl.when(pl.program_id(1) == 0)
    def _():
        state_ref[...] = jnp.zeros_like(state_ref)
        ext_ref[0:8, :] = jnp.zeros((8, 3 * width), F32)

    ext_ref[8:8 + c, :] = x_ref[:, 0:3 * width].astype(F32)

    def conv_silu(col):
        sl = slice(col, col + HEAD_DIM)
        y = cw_ref[DN_CONV - 1:DN_CONV, sl] * ext_ref[8:8 + c, sl]
        for j in range(DN_CONV - 1):
            y = y + cw_ref[j:j + 1, sl] * ext_ref[5 + j:5 + j + c, sl]
        return _silu(y)

    ab = ab_ref[...]
    g_all = -jnp.exp(alog_ref[...]) * _softplus(ab + dtb_ref[...])
    beta_all = _sigmoid(ab)
    row = lax.broadcasted_iota(jnp.int32, (c, c), 0)
    col = lax.broadcasted_iota(jnp.int32, (c, c), 1)
    causal = row >= col
    strict = row > col
    eye = (row == col).astype(F32)
    g_parts = _split3(g_all)
    prefix_ones = causal.astype(BF16)
    all_ones = jnp.ones((c, c), BF16)
    gc_all = sum(_dot(prefix_ones, p) for p in g_parts)
    gt_all = sum(_dot(all_ones, p) for p in g_parts)
    gc_rows = gc_all.T
    levels = []
    b = 1
    while b < c:
        shift = b.bit_length() - 1
        levels.append(((row >> (shift + 1)) == (col >> (shift + 1)))
                      & (((row >> shift) & 1) == 1) & (((col >> shift) & 1) == 0))
        b *= 2

    heads = range(N_HEADS)
    q, k, v = [], [], []
    for h in heads:
        qh = conv_silu(h * HEAD_DIM)
        kh = conv_silu(width + h * HEAD_DIM)
        q.append(qh * lax.rsqrt(jnp.sum(qh * qh, axis=-1, keepdims=True) + NORM_EPS) * (HEAD_DIM ** -0.5))
        k.append(kh * lax.rsqrt(jnp.sum(kh * kh, axis=-1, keepdims=True) + NORM_EPS))
        v.append(conv_silu(2 * width + h * HEAD_DIM))
    g_col = [gc_all[:, h:h + 1] for h in heads]
    g_tot = [gt_all[:, h:h + 1] for h in heads]
    beta = [beta_all[:, N_HEADS + h:N_HEADS + h + 1] for h in heads]
    decay = [jnp.exp(jnp.where(causal, g_col[h] - gc_rows[h:h + 1, :], 0.0)) for h in heads]
    k_bf = [k[h].astype(BF16) for h in heads]
    k_beta = [k[h] * beta[h] for h in heads]
    kk = [_dot_nt(k_beta[h].astype(BF16), k_bf[h]) for h in heads]
    qk = [_dot_nt(q[h].astype(BF16), k_bf[h]) for h in heads]
    lower = [jnp.where(strict, kk[h] * decay[h], 0.0) for h in heads]
    inv = [eye - jnp.where(levels[0], lower[h], 0.0) for h in heads]
    lower_bf = [lower[h].astype(BF16) for h in heads]
    for mask in levels[1:]:
        mask_bf = mask.astype(BF16)
        inv_bf = [inv[h].astype(BF16) for h in heads]
        step = [_dot(inv_bf[h], lower_bf[h] * mask_bf) for h in heads]
        inv = [inv[h] - _dot(step[h].astype(BF16), inv_bf[h]) for h in heads]
    exp_g = [jnp.exp(g_col[h]) for h in heads]
    uw = [_dot(inv[h].astype(BF16),
               jnp.concatenate([v[h] * beta[h], k_beta[h] * exp_g[h]], axis=1).astype(BF16)) for h in heads]
    state = [state_ref[h] for h in heads]
    ws_qs = [_dot(jnp.concatenate([uw[h][:, HEAD_DIM:], q[h] * exp_g[h]], axis=0).astype(BF16),
                  state[h].astype(BF16)) for h in heads]
    v_new = [(uw[h][:, :HEAD_DIM] - ws_qs[h][:c]).astype(BF16) for h in heads]
    attn = [jnp.where(causal, qk[h] * decay[h], 0.0).astype(BF16) for h in heads]
    out = [ws_qs[h][c:] + _dot(attn[h], v_new[h]) for h in heads]
    for h in heads:
        k_dec = k[h] * jnp.exp(g_tot[h] - g_col[h])
        state_ref[h] = state[h] * jnp.exp(g_tot[h]) + _dot(k_dec.T.astype(BF16), v_new[h])
    for h in heads:
        z = x_ref[:, 3 * width + h * HEAD_DIM:3 * width + (h + 1) * HEAD_DIM].astype(F32)
        o_ref[:, h * HEAD_DIM:(h + 1) * HEAD_DIM] = (
            _rms(out[h]) * onorm_ref[...] * _silu(z)).astype(o_ref.dtype)

    ext_ref[0:8, :] = ext_ref[c:c + 8, :]


def _dn_core(qkvz, ab, conv_w, a_log, dt_bias, out_norm, *, batch, seq):
    m = qkvz.shape[0]
    nc = seq // DN_CHUNK
    width = N_HEADS * HEAD_DIM
    full = lambda a: pl.BlockSpec(a.shape, lambda b, t: (0, 0))
    return pl.pallas_call(
        _dn_body,
        grid=(batch, nc),
        in_specs=[pl.BlockSpec((DN_CHUNK, 4 * width), lambda b, t: (b * nc + t, 0)),
                  pl.BlockSpec((DN_CHUNK, LANES), lambda b, t: (b * nc + t, 0)),
                  full(conv_w), full(a_log), full(dt_bias), full(out_norm)],
        out_specs=pl.BlockSpec((DN_CHUNK, width), lambda b, t: (b * nc + t, 0)),
        out_shape=jax.ShapeDtypeStruct((m, width), BF16),
        scratch_shapes=[pltpu.VMEM((N_HEADS, HEAD_DIM, HEAD_DIM), F32),
                        pltpu.VMEM((DN_CHUNK + 8, 3 * width), F32)],
        compiler_params=_params("parallel", "arbitrary"),
        name="dn_core",
    )(qkvz, ab, conv_w, a_log, dt_bias, out_norm)


def _pad_lanes(vec, n=LANES):
    return jnp.pad(vec.astype(F32), (0, n - vec.shape[0])).reshape(1, n)


def _deltanet_mixer(x, mix_norm, w_in, conv_w, a_log, dt_bias, out_norm, *, batch, seq):
    width = N_HEADS * HEAD_DIM
    w_main = w_in[:, :4 * width].astype(BF16)
    w_gate = jnp.pad(w_in[:, 4 * width:], ((0, 0), (0, LANES - 2 * N_HEADS))).astype(BF16)
    qkvz, ab = _dn_proj(x, mix_norm, w_main, w_gate, tm=1024, tn=2048)
    o = _dn_core(qkvz, ab, conv_w, _pad_lanes(a_log), _pad_lanes(dt_bias), out_norm.reshape(1, HEAD_DIM),
                 batch=batch, seq=seq)
    return o


def _stick_breaking_mixer(x, mix_norm, w_qkv, q_norm, k_norm, *, batch, seq):
    gains = (q_norm.reshape(1, HEAD_DIM), k_norm.reshape(1, HEAD_DIM))
    qkv = _norm_matmul(x, mix_norm, w_qkv.astype(BF16), tm=1024, tn=N_HEADS * HEAD_DIM, out_dtype=BF16,
                       epilogue=_sb_epilogue, extra=gains)
    o = _sb_attention(qkv, *gains, batch=batch, seq=seq, tile=256, nsub=4, kunroll=1)
    return o


def _rope_tables(seq):
    half = MLA_ROPE // 2
    inv_freq = ROPE_THETA ** (-jnp.arange(0, MLA_ROPE, 2, dtype=F32) / MLA_ROPE)
    ang = jnp.arange(seq, dtype=F32)[:, None] * inv_freq[None, :]
    cos, sin, zero = jnp.cos(ang), jnp.sin(ang), jnp.zeros((seq, half), F32)
    pad = jnp.zeros((seq, LANES - MLA_ROPE), F32)
    return (jnp.concatenate([cos, cos, pad], axis=1),
            jnp.concatenate([-sin, zero, pad], axis=1),
            jnp.concatenate([zero, sin, pad], axis=1))


def _mla_mixer(x, mix_norm, w_down, q_a_norm, kv_a_norm, w_uq, w_ukv, q_nope_norm, q_rope_norm,
               k_nope_norm, k_rope_norm, *, batch, seq):
    down_w = jnp.pad(w_down, ((0, 0), (0, 512 - w_down.shape[1]))).astype(BF16)
    wq = jnp.pad(w_uq.reshape(MLA_Q_RANK, N_HEADS, MLA_QK),
                 ((0, 0), (0, 0), (0, MLA_QPAD - MLA_QK))).reshape(MLA_Q_RANK, N_HEADS * MLA_QPAD).astype(BF16)
    wkv = w_ukv.reshape(MLA_KV_RANK, N_HEADS, 2, HEAD_DIM).transpose(0, 2, 1, 3).reshape(
        MLA_KV_RANK, 2 * N_HEADS * HEAD_DIM).astype(BF16)
    down = _norm_matmul(x, mix_norm, down_w, tm=1024, tn=512, out_dtype=F32)
    gains = (q_a_norm.reshape(1, -1), kv_a_norm.reshape(1, -1), q_nope_norm.reshape(1, -1),
             _pad_lanes(q_rope_norm), k_nope_norm.reshape(1, -1), _pad_lanes(k_rope_norm))
    q, kn, v, kr = _mla_prep(down, wq, wkv, gains, _rope_tables(seq), seq=seq, tm=512)
    attend = functools.partial(_mla_attention, q, kn, kr, v, batch=batch, seq=seq, tile=256, nsub=4,
                               kunroll=4)
    bound = _mla_score_bound(gains[2], gains[3], gains[4], gains[5])[0, 0]
    o = lax.cond(bound < MLA_MAX_BOUND, lambda: attend(bounded=True), lambda: attend(bounded=False))
    return o


def kernel(x, l0_mix_norm, l0_dn_w_in, l0_dn_conv_w, l0_dn_a_log, l0_dn_dt_bias, l0_dn_out_norm, l0_dn_w_out, l0_ffn_norm, l0_ffn_w_gate_up, l0_ffn_w_down, l1_mix_norm, l1_sb_w_qkv, l1_sb_q_norm, l1_sb_k_norm, l1_sb_w_out, l1_ffn_norm, l1_ffn_w_gate_up, l1_ffn_w_down, l2_mix_norm, l2_mla_w_down, l2_mla_q_a_norm, l2_mla_kv_a_norm, l2_mla_w_uq, l2_mla_w_ukv, l2_mla_q_nope_norm, l2_mla_q_rope_norm, l2_mla_k_nope_norm, l2_mla_k_rope_norm, l2_mla_w_out, l2_ffn_norm, l2_ffn_w_gate_up, l2_ffn_w_down, l3_mix_norm, l3_dn_w_in, l3_dn_conv_w, l3_dn_a_log, l3_dn_dt_bias, l3_dn_out_norm, l3_dn_w_out, l3_ffn_norm, l3_ffn_w_gate_up, l3_ffn_w_down):
    batch, seq, d_model = x.shape
    shape = dict(batch=batch, seq=seq)
    def close_layer(mixed, w_out, y, g, wgu, wd):
        return _proj_ffn(mixed, w_out.astype(BF16), y, g, wgu.astype(BF16), wd.astype(BF16),
                         tm=FFN_ROWS, th=wd.shape[0])

    y = x.reshape(batch * seq, d_model)
    o = _deltanet_mixer(y, l0_mix_norm, l0_dn_w_in, l0_dn_conv_w, l0_dn_a_log, l0_dn_dt_bias, l0_dn_out_norm,
                        **shape)
    y = close_layer(o, l0_dn_w_out, y, l0_ffn_norm, l0_ffn_w_gate_up, l0_ffn_w_down)
    o = _stick_breaking_mixer(y, l1_mix_norm, l1_sb_w_qkv, l1_sb_q_norm, l1_sb_k_norm, **shape)
    y = close_layer(o, l1_sb_w_out, y, l1_ffn_norm, l1_ffn_w_gate_up, l1_ffn_w_down)
    o = _mla_mixer(y, l2_mix_norm, l2_mla_w_down, l2_mla_q_a_norm, l2_mla_kv_a_norm, l2_mla_w_uq, l2_mla_w_ukv,
                   l2_mla_q_nope_norm, l2_mla_q_rope_norm, l2_mla_k_nope_norm, l2_mla_k_rope_norm, **shape)
    y = close_layer(o, l2_mla_w_out, y, l2_ffn_norm, l2_ffn_w_gate_up, l2_ffn_w_down)
    o = _deltanet_mixer(y, l3_mix_norm, l3_dn_w_in, l3_dn_conv_w, l3_dn_a_log, l3_dn_dt_bias, l3_dn_out_norm,
                        **shape)
    y = close_layer(o, l3_dn_w_out, y, l3_ffn_norm, l3_ffn_w_gate_up, l3_ffn_w_down)
    return y.reshape(batch, seq, d_model)
```
